```python
import math
import jax, jax.numpy as jnp
from jax import lax
import numpy as np

D_MODEL = 4096
BATCH = 4
SEQ = 2048
DEPTH = 4
DEC_BATCH = 8
DEC_SEQ = 1
PAST_LEN = 8192
PAGE_SIZE = 128

N_MIXERS = 4
N_SSD_LAYERS = (DEPTH + 3) // 4
N_FOX_LAYERS = (DEPTH + 2) // 4
N_MOBA_LAYERS = (DEPTH + 1) // 4
N_GMLP_LAYERS = DEPTH // 4
NORM_EPS = 1e-6

SSD_D_INNER = 2 * D_MODEL
SSD_HEAD_DIM = 64
SSD_HEADS = SSD_D_INNER // SSD_HEAD_DIM
SSD_GROUPS = 8
SSD_STATE = 128
SSD_CONV = 4
SSD_CONV_DIM = SSD_D_INNER + 2 * SSD_GROUPS * SSD_STATE
SSD_IN = SSD_D_INNER + SSD_CONV_DIM + SSD_HEADS
SSD_CHUNK = 128

FOX_HEAD_DIM = 128
FOX_HEADS = D_MODEL // FOX_HEAD_DIM
FOX_SCALE = FOX_HEAD_DIM ** -0.5
ATTN_QBLOCK = 128

MOBA_HEAD_DIM = 128
MOBA_HEADS = D_MODEL // MOBA_HEAD_DIM
MOBA_SCALE = MOBA_HEAD_DIM ** -0.5
MOBA_BLOCK = 256
MOBA_TOPK = 3
MOBA_QBLOCK = 8

REL_BUCKETS = 32
REL_MAX_DIST = 128

GMLP_CHUNK = 128
GMLP_HALF = 2 * D_MODEL
GMLP_GROUPS = 16

D_FF = ((8 * D_MODEL // 3 + 255) // 256) * 256
FFN_CONV = 3

kernel_name = 'hybrid_ssd_fox_moba_gmlp_decode_step'


def rmsnorm(x, g):
    x32 = x.astype(jnp.float32)
    y = x32 * lax.rsqrt(jnp.mean(x32 * x32, axis=-1, keepdims=True) + NORM_EPS)
    return (y * g.astype(jnp.float32)).astype(x.dtype)


def layernorm(x, g, b):
    x32 = x.astype(jnp.float32)
    xc = x32 - jnp.mean(x32, axis=-1, keepdims=True)
    y = xc * lax.rsqrt(jnp.mean(xc * xc, axis=-1, keepdims=True) + NORM_EPS)
    return (y * g.astype(jnp.float32) + b.astype(jnp.float32)).astype(x.dtype)


def causal_dwconv(x, prev, w, b):
    width, length = w.shape[0], x.shape[1]
    cat = jnp.concatenate([prev.astype(x.dtype), x], axis=1)
    y = sum(cat[:, k:k + length] * w[k] for k in range(width)) + b
    return y, cat[:, length:]


def gather_pages(pool, page_table):
    g = pool[page_table]
    return g.reshape((g.shape[0], g.shape[1] * g.shape[2]) + g.shape[3:])


def ssd_scan(x, dt, a, bmat, cmat, state0):
    f32 = jnp.float32
    bsz, length, n_h, p_dim = x.shape
    n_g, n_s = bmat.shape[2], bmat.shape[3]
    hpg = n_h // n_g
    c = math.gcd(length, SSD_CHUNK)
    nc = length // c
    tri = jnp.tril(jnp.ones((c, c), bool))[None, :, :, None]

    def to_chunks(t):
        return t.astype(f32).reshape((bsz, nc, c) + t.shape[2:]).swapaxes(0, 1)

    def step(state, inp):
        xc, dtc, bc, cc = inp
        cum = jnp.cumsum(dtc * a, axis=1)
        seg = cum[:, :, None, :] - cum[:, None, :, :]
        decay = jnp.where(tri, jnp.exp(jnp.where(tri, seg, 0.0)), 0.0)
        cb = jnp.einsum('btgn,bsgn->btsg', cc, bc)
        w = cb[..., None] * (decay * dtc[:, None, :, :]).reshape(bsz, c, c, n_g, hpg)
        xg = xc.reshape(bsz, c, n_g, hpg, p_dim)
        sg = state.reshape(bsz, n_g, hpg, p_dim, n_s)
        y = (jnp.einsum('btsgh,bsghp->btghp', w, xg)
             + jnp.einsum('btgn,bghpn->btghp', cc, sg) * jnp.exp(cum).reshape(bsz, c, n_g, hpg)[..., None])
        to_end = (jnp.exp(cum[:, -1:, :] - cum) * dtc).reshape(bsz, c, n_g, hpg)
        new = (sg * jnp.exp(cum[:, -1, :]).reshape(bsz, n_g, hpg)[..., None, None]
               + jnp.einsum('bsgh,bsghp,bsgn->bghpn', to_end, xg, bc))
        return new.reshape(bsz, n_h, p_dim, n_s), y.reshape(bsz, c, n_h, p_dim)

    final, ys = lax.scan(step, state0.astype(f32), (to_chunks(x), to_chunks(dt), to_chunks(bmat), to_chunks(cmat)))
    return ys.swapaxes(0, 1).reshape(bsz, length, n_h, p_dim), final


def ssd_mixer(h, conv_prev, ssm_prev, w_in, conv_w, conv_b, dt_bias, a_log, d_skip, norm_w, w_out):
    f32 = jnp.float32
    bsz, length, _ = h.shape
    gn = SSD_GROUPS * SSD_STATE
    proj = h @ w_in
    z = proj[..., :SSD_D_INNER]
    xbc = proj[..., SSD_D_INNER:SSD_D_INNER + SSD_CONV_DIM]
    dt = proj[..., SSD_D_INNER + SSD_CONV_DIM:]
    xbc, conv_new = causal_dwconv(xbc, conv_prev, conv_w, conv_b)
    xbc = jax.nn.silu(xbc)
    xs = xbc[..., :SSD_D_INNER].reshape(bsz, length, SSD_HEADS, SSD_HEAD_DIM)
    bm = xbc[..., SSD_D_INNER:SSD_D_INNER + gn].reshape(bsz, length, SSD_GROUPS, SSD_STATE)
    cm = xbc[..., SSD_D_INNER + gn:].reshape(bsz, length, SSD_GROUPS, SSD_STATE)
    dt = jax.nn.softplus(dt.astype(f32) + dt_bias.astype(f32))
    a = -jnp.exp(a_log.astype(f32))
    y, ssm_new = ssd_scan(xs, dt, a, bm, cm, ssm_prev)
    y = y + d_skip.astype(f32)[:, None] * xs.astype(f32)
    g = (y.reshape(bsz, length, SSD_D_INNER) * jax.nn.silu(z.astype(f32))).reshape(bsz, length, SSD_GROUPS, -1)
    g = g * lax.rsqrt(jnp.mean(g * g, axis=-1, keepdims=True) + NORM_EPS)
    g = (g.reshape(bsz, length, SSD_D_INNER) * norm_w.astype(f32)).astype(h.dtype)
    return g @ w_out, conv_new, ssm_new


def fox_attend(q, k, v, logf, q_offset):
    bsz, lq, n_h, d_h = q.shape
    lk = k.shape[1]
    cum = jnp.cumsum(logf.astype(jnp.float32), axis=1)
    cum_q = cum[:, q_offset:q_offset + lq]
    cum_k = cum.transpose(0, 2, 1)[:, :, None, :]
    kpos = jnp.arange(lk)
    qb = math.gcd(lq, ATTN_QBLOCK)
    nb = lq // qb

    def block(args):
        qi, ci, pi = args
        s = (jnp.einsum('bqhd,bkhd->bhqk', qi, k).astype(jnp.float32) * FOX_SCALE
             + ci.transpose(0, 2, 1)[..., None] - cum_k)
        s = jnp.where(pi[:, None] >= kpos[None, :], s, -jnp.inf)
        p = jax.nn.softmax(s, axis=-1).astype(v.dtype)
        return jnp.einsum('bhqk,bkhd->bqhd', p, v)

    xs = (q.reshape(bsz, nb, qb, n_h, d_h).swapaxes(0, 1),
          cum_q.reshape(bsz, nb, qb, n_h).swapaxes(0, 1),
          (q_offset + jnp.arange(lq)).reshape(nb, qb))
    o = lax.map(block, xs)
    return o.swapaxes(0, 1).reshape(bsz, lq, n_h, d_h)


def fox_mixer(h, past, w_qkv, w_f, b_f, w_o):
    bsz, length, _ = h.shape
    qkv = (h @ w_qkv).reshape(bsz, length, 3, FOX_HEADS, FOX_HEAD_DIM)
    q, k, v = qkv[:, :, 0], qkv[:, :, 1], qkv[:, :, 2]
    logf = jax.nn.log_sigmoid((h @ w_f).astype(jnp.float32) + b_f.astype(jnp.float32))
    if past is None:
        k_all, v_all, lf_all, off = k, v, logf, 0
    else:
        k_all = jnp.concatenate([past[0].astype(k.dtype), k], axis=1)
        v_all = jnp.concatenate([past[1].astype(v.dtype), v], axis=1)
        lf_all = jnp.concatenate([past[2].astype(jnp.float32), logf], axis=1)
        off = past[0].shape[1]
    o = fox_attend(q, k_all, v_all, lf_all, off)
    return o.reshape(bsz, length, -1) @ w_o, k, v, logf


def t5_bucket(rel):
    n = jnp.maximum(rel, 0)
    exact = REL_BUCKETS // 2
    nf = jnp.maximum(n, 1).astype(jnp.float32)
    large = exact + (jnp.log(nf / exact) / math.log(REL_MAX_DIST / exact) * (REL_BUCKETS - exact)).astype(jnp.int32)
    return jnp.where(n < exact, n, jnp.minimum(large, REL_BUCKETS - 1))


def moba_attend(q, k, v, q_offset, rel_bias):
    f32 = jnp.float32
    bsz, lq, n_h, d_h = q.shape
    lk = k.shape[1]
    nblk = -(-lk // MOBA_BLOCK)
    pad = nblk * MOBA_BLOCK - lk
    if pad:
        k = jnp.pad(k, ((0, 0), (0, pad), (0, 0), (0, 0)))
        v = jnp.pad(v, ((0, 0), (0, pad), (0, 0), (0, 0)))
    kblk = k.reshape(bsz, nblk, MOBA_BLOCK, n_h, d_h)
    vblk = v.reshape(bsz, nblk, MOBA_BLOCK, n_h, d_h)
    kmean = jnp.mean(kblk.astype(f32), axis=2)
    qpos = q_offset + jnp.arange(lq)
    qblk = qpos // MOBA_BLOCK
    gate = jnp.einsum('bqhd,bnhd->bhqn', q.astype(f32), kmean)
    gate = jnp.where(jnp.arange(nblk)[None, :] < qblk[:, None], gate, -jnp.inf)
    top_val, top_idx = lax.top_k(gate, min(MOBA_TOPK, nblk))
    sel = jnp.concatenate([top_idx.astype(jnp.int32),
                           jnp.broadcast_to(qblk[None, None, :, None], (bsz, n_h, lq, 1)).astype(jnp.int32)], axis=-1)
    ok = jnp.concatenate([jnp.isfinite(top_val), jnp.ones((bsz, n_h, lq, 1), bool)], axis=-1)
    n_sel = sel.shape[-1]
    qb = math.gcd(lq, MOBA_QBLOCK)
    nq = lq // qb
    bi = jnp.arange(bsz)[:, None, None, None]
    hi = jnp.arange(n_h)[None, :, None, None]
    bias_hb = rel_bias.T.astype(f32)
    roff = jnp.arange(MOBA_BLOCK)

    def block(args):
        qi, si, oki, pi = args
        kg = kblk[bi, si, :, hi]
        vg = vblk[bi, si, :, hi]
        rel = pi[None, None, :, None, None] - (si[..., None] * MOBA_BLOCK + roff)
        bias = bias_hb[hi[..., None], t5_bucket(rel)]
        s = jnp.einsum('bqhd,bhqjrd->bhqjr', qi, kg).astype(f32) * MOBA_SCALE + bias
        s = jnp.where(oki[..., None] & (rel >= 0), s, -jnp.inf)
        p = jax.nn.softmax(s.reshape(bsz, n_h, qb, n_sel * MOBA_BLOCK), axis=-1).reshape(s.shape)
        return jnp.einsum('bhqjr,bhqjrd->bqhd', p.astype(vg.dtype), vg)

    xs = (q.reshape(bsz, nq, qb, n_h, d_h).swapaxes(0, 1),
          sel.reshape(bsz, n_h, nq, qb, n_sel).transpose(2, 0, 1, 3, 4),
          ok.reshape(bsz, n_h, nq, qb, n_sel).transpose(2, 0, 1, 3, 4),
          qpos.reshape(nq, qb))
    o = lax.map(block, xs)
    return o.swapaxes(0, 1).reshape(bsz, lq, n_h, d_h)


def moba_mixer(h, past, w_qkv, w_o, rel_bias):
    bsz, length, _ = h.shape
    qkv = (h @ w_qkv).reshape(bsz, length, 3, MOBA_HEADS, MOBA_HEAD_DIM)
    q, k, v = qkv[:, :, 0], qkv[:, :, 1], qkv[:, :, 2]
    if past is None:
        k_all, v_all, off = k, v, 0
    else:
        k_all = jnp.concatenate([past[0].astype(k.dtype), k], axis=1)
        v_all = jnp.concatenate([past[1].astype(v.dtype), v], axis=1)
        off = past[0].shape[1]
    o = moba_attend(q, k_all, v_all, off, rel_bias)
    return o.reshape(bsz, length, -1) @ w_o, k, v


def gmlp_mixer(h, w_in, b_in, ln_g, ln_b, w_sp, b_sp, w_out):
    bsz, length, _ = h.shape
    z = jax.nn.gelu(h @ w_in + b_in)
    u, v = z[..., :GMLP_HALF], z[..., GMLP_HALF:]
    v = layernorm(v, ln_g, ln_b)
    n_chunk = -(-length // GMLP_CHUNK)
    pad = n_chunk * GMLP_CHUNK - length
    vp = jnp.pad(v, ((0, 0), (0, pad), (0, 0))).reshape(bsz, n_chunk, GMLP_CHUNK, GMLP_GROUPS, GMLP_HALF // GMLP_GROUPS)
    w_causal = jnp.where(jnp.tril(jnp.ones((GMLP_CHUNK, GMLP_CHUNK), bool))[None], w_sp, 0.0).astype(v.dtype)
    s = jnp.einsum('gij,bcjgd->bcigd', w_causal, vp) + b_sp.T.astype(v.dtype)[None, None, :, :, None]
    s = s.reshape(bsz, n_chunk * GMLP_CHUNK, GMLP_HALF)[:, :length]
    return (u * s) @ w_out, v


def conv_ffn(h, prev, w_up, conv_w, conv_b, w_down):
    up = h @ w_up
    up_c, new_prev = causal_dwconv(up, prev, conv_w, conv_b)
    gate, val = up_c[..., :D_FF], up_c[..., D_FF:]
    return (jax.nn.silu(gate) * val) @ w_down, new_prev


def setup_inputs(seed: int = 0) -> dict:
    key = jax.random.key(seed)
    ks = iter(jax.random.split(key, 64))
    f32 = jnp.float32

    def rn(shape, scale=1.0):
        return jax.random.normal(next(ks), shape, f32) * scale

    n_pages = PAST_LEN // PAGE_SIZE
    n_pool = (5 * DEC_BATCH * n_pages) // 4
    page_table = jax.random.permutation(next(ks), n_pool)[:DEC_BATCH * n_pages].reshape(DEC_BATCH, n_pages).astype(jnp.int32)
    dt0 = jnp.exp(jax.random.uniform(next(ks), (N_SSD_LAYERS, SSD_HEADS), f32, math.log(1e-3), math.log(1e-1)))
    dt_bias = dt0 + jnp.log(-jnp.expm1(-dt0))
    a_log = jnp.log(jax.random.uniform(next(ks), (N_SSD_LAYERS, SSD_HEADS), f32, 1.0, 16.0))
    return {
        'x_prompt': rn((BATCH, SEQ, D_MODEL)),
        'x_sample': rn((DEC_BATCH, DEC_SEQ, D_MODEL)),
        'state_ssd_conv': rn((N_SSD_LAYERS, DEC_BATCH, SSD_CONV - 1, SSD_CONV_DIM)),
        'state_ssd': rn((N_SSD_LAYERS, DEC_BATCH, SSD_HEADS, SSD_HEAD_DIM, SSD_STATE), 0.5),
        'cache_fox_k': rn((N_FOX_LAYERS, n_pool, PAGE_SIZE, FOX_HEADS, FOX_HEAD_DIM)),
        'cache_fox_v': rn((N_FOX_LAYERS, n_pool, PAGE_SIZE, FOX_HEADS, FOX_HEAD_DIM)),
        'cache_fox_logf': jax.nn.log_sigmoid(3.0 + rn((N_FOX_LAYERS, n_pool, PAGE_SIZE, FOX_HEADS))),
        'cache_moba_k': rn((N_MOBA_LAYERS, n_pool, PAGE_SIZE, MOBA_HEADS, MOBA_HEAD_DIM)),
        'cache_moba_v': rn((N_MOBA_LAYERS, n_pool, PAGE_SIZE, MOBA_HEADS, MOBA_HEAD_DIM)),
        'state_ffn_conv': rn((DEPTH, DEC_BATCH, FFN_CONV - 1, 2 * D_FF)),
        'page_table': page_table,
        'norm_mix': 1.0 + rn((DEPTH, D_MODEL), 0.02),
        'norm_ffn': 1.0 + rn((DEPTH, D_MODEL), 0.02),
        'norm_final': 1.0 + rn((D_MODEL,), 0.02),
        'ssd_w_in': rn((N_SSD_LAYERS, D_MODEL, SSD_IN), D_MODEL ** -0.5),
        'ssd_conv_w': rn((N_SSD_LAYERS, SSD_CONV, SSD_CONV_DIM), SSD_CONV ** -0.5),
        'ssd_conv_b': rn((N_SSD_LAYERS, SSD_CONV_DIM), 0.01),
        'ssd_dt_bias': dt_bias,
        'ssd_a_log': a_log,
        'ssd_d': 1.0 + rn((N_SSD_LAYERS, SSD_HEADS), 0.1),
        'ssd_norm': 1.0 + rn((N_SSD_LAYERS, SSD_D_INNER), 0.02),
        'ssd_w_out': rn((N_SSD_LAYERS, SSD_D_INNER, D_MODEL), SSD_D_INNER ** -0.5),
        'fox_w_qkv': rn((N_FOX_LAYERS, D_MODEL, 3 * FOX_HEADS * FOX_HEAD_DIM), D_MODEL ** -0.5),
        'fox_w_f': rn((N_FOX_LAYERS, D_MODEL, FOX_HEADS), D_MODEL ** -0.5),
        'fox_b_f': 3.0 + rn((N_FOX_LAYERS, FOX_HEADS), 0.1),
        'fox_w_o': rn((N_FOX_LAYERS, FOX_HEADS * FOX_HEAD_DIM, D_MODEL), (FOX_HEADS * FOX_HEAD_DIM) ** -0.5),
        'moba_w_qkv': rn((N_MOBA_LAYERS, D_MODEL, 3 * MOBA_HEADS * MOBA_HEAD_DIM), D_MODEL ** -0.5),
        'moba_w_o': rn((N_MOBA_LAYERS, MOBA_HEADS * MOBA_HEAD_DIM, D_MODEL), (MOBA_HEADS * MOBA_HEAD_DIM) ** -0.5),
        'rel_bias': rn((REL_BUCKETS, MOBA_HEADS), 0.5),
        'gmlp_w_in': rn((N_GMLP_LAYERS, D_MODEL, 2 * GMLP_HALF), D_MODEL ** -0.5),
        'gmlp_b_in': rn((N_GMLP_LAYERS, 2 * GMLP_HALF), 0.01),
        'gmlp_ln_g': 1.0 + rn((N_GMLP_LAYERS, GMLP_HALF), 0.02),
        'gmlp_ln_b': rn((N_GMLP_LAYERS, GMLP_HALF), 0.01),
        'gmlp_w_sp': rn((N_GMLP_LAYERS, GMLP_GROUPS, GMLP_CHUNK, GMLP_CHUNK), 0.5 * GMLP_CHUNK ** -0.5),
        'gmlp_b_sp': 1.0 + rn((N_GMLP_LAYERS, GMLP_GROUPS, GMLP_CHUNK), 0.1),
        'gmlp_w_out': rn((N_GMLP_LAYERS, GMLP_HALF, D_MODEL), GMLP_HALF ** -0.5),
        'ffn_w_up': rn((DEPTH, D_MODEL, 2 * D_FF), D_MODEL ** -0.5),
        'ffn_conv_w': rn((DEPTH, FFN_CONV, 2 * D_FF), FFN_CONV ** -0.5),
        'ffn_conv_b': rn((DEPTH, 2 * D_FF), 0.01),
        'ffn_w_down': rn((DEPTH, D_FF, D_MODEL), D_FF ** -0.5),
    }


def reference(x_prompt, x_sample, state_ssd_conv, state_ssd, cache_fox_k, cache_fox_v, cache_fox_logf,
              cache_moba_k, cache_moba_v, state_ffn_conv, page_table, norm_mix, norm_ffn, norm_final,
              ssd_w_in, ssd_conv_w, ssd_conv_b, ssd_dt_bias, ssd_a_log, ssd_d, ssd_norm, ssd_w_out,
              fox_w_qkv, fox_w_f, fox_b_f, fox_w_o, moba_w_qkv, moba_w_o, rel_bias,
              gmlp_w_in, gmlp_b_in, gmlp_ln_g, gmlp_ln_b, gmlp_w_sp, gmlp_b_sp, gmlp_w_out,
              ffn_w_up, ffn_conv_w, ffn_conv_b, ffn_w_down):

    def run_group(x, sample):
        bsz = x.shape[0]
        ssd_conv, ssd_state, fox_k, fox_v, fox_lf, moba_k, moba_v, gmlp_v, ffn_conv = ([] for _ in range(9))
        for i in range(DEPTH):
            m, j = i % N_MIXERS, i // N_MIXERS
            h = rmsnorm(x, norm_mix[i])
            if m == 0:
                if sample:
                    c_prev, s_prev = state_ssd_conv[j], state_ssd[j]
                else:
                    c_prev = jnp.zeros((bsz, SSD_CONV - 1, SSD_CONV_DIM), x.dtype)
                    s_prev = jnp.zeros((bsz, SSD_HEADS, SSD_HEAD_DIM, SSD_STATE), jnp.float32)
                out, c_new, s_new = ssd_mixer(h, c_prev, s_prev, ssd_w_in[j], ssd_conv_w[j], ssd_conv_b[j],
                                              ssd_dt_bias[j], ssd_a_log[j], ssd_d[j], ssd_norm[j], ssd_w_out[j])
                ssd_conv.append(c_new)
                ssd_state.append(s_new)
            elif m == 1:
                past = None
                if sample:
                    past = (gather_pages(cache_fox_k[j], page_table), gather_pages(cache_fox_v[j], page_table),
                            gather_pages(cache_fox_logf[j], page_table))
                out, k_new, v_new, lf_new = fox_mixer(h, past, fox_w_qkv[j], fox_w_f[j], fox_b_f[j], fox_w_o[j])
                fox_k.append(k_new)
                fox_v.append(v_new)
                fox_lf.append(lf_new)
            elif m == 2:
                past = None
                if sample:
                    past = (gather_pages(cache_moba_k[j], page_table), gather_pages(cache_moba_v[j], page_table))
                out, k_new, v_new = moba_mixer(h, past, moba_w_qkv[j], moba_w_o[j], rel_bias)
                moba_k.append(k_new)
                moba_v.append(v_new)
            else:
                out, v_rows = gmlp_mixer(h, gmlp_w_in[j], gmlp_b_in[j], gmlp_ln_g[j], gmlp_ln_b[j],
                                         gmlp_w_sp[j], gmlp_b_sp[j], gmlp_w_out[j])
                if sample:
                    gmlp_v.append(v_rows)
            x = x + out
            f_prev = state_ffn_conv[i] if sample else jnp.zeros((bsz, FFN_CONV - 1, 2 * D_FF), x.dtype)
            out, f_new = conv_ffn(rmsnorm(x, norm_ffn[i]), f_prev, ffn_w_up[i], ffn_conv_w[i], ffn_conv_b[i], ffn_w_down[i])
            x = x + out
            ffn_conv.append(f_new)
        y = rmsnorm(x, norm_final)
        stacked = [jnp.stack(t) for t in (ssd_conv, ssd_state, fox_k, fox_v, fox_lf, moba_k, moba_v, ffn_conv)]
        return y, stacked, gmlp_v

    y_prompt, p_states, _ = run_group(x_prompt, False)
    p_ssd_conv, p_ssd_state, p_fox_k, p_fox_v, p_fox_logf, p_moba_k, p_moba_v, p_ffn_conv = p_states
    y_sample, s_states, s_gmlp = run_group(x_sample, True)
    s_ssd_conv, s_ssd_state, s_fox_k, s_fox_v, s_fox_logf, s_moba_k, s_moba_v, s_ffn_conv = s_states
    s_gmlp_v = jnp.stack(s_gmlp)
    return (y_prompt, y_sample,
            p_ssd_conv, p_ssd_state, p_fox_k, p_fox_v, p_fox_logf, p_moba_k, p_moba_v, p_ffn_conv,
            s_ssd_conv, s_ssd_state, s_fox_k, s_fox_v, s_fox_logf, s_moba_k, s_moba_v, s_gmlp_v, s_ffn_conv)
```

```python
import functools
import math

import jax
import jax.numpy as jnp
from jax import lax
from jax.experimental import pallas as pl
from jax.experimental.pallas import tpu as pltpu

F32 = jnp.float32
BF16 = jnp.bfloat16

NORM_EPS = 1e-6
SSD_CHUNK = 128
MOBA_BLOCK = 256
MOBA_TOPK = 3
REL_MAX_DIST = 128
LANES = 128
SUBLANES = 8
VMEM_LIMIT = 56 * 1024 * 1024
MM_BUDGET = 46 * 1024 * 1024
NEG_INF = float("-inf")


def _params(*sem):
    return pltpu.CompilerParams(dimension_semantics=sem, vmem_limit_bytes=VMEM_LIMIT)


def _dot(a, b):
    return jnp.dot(a, b, preferred_element_type=F32)


def _dot_nt(a, b):
    return lax.dot_general(a, b, (((1,), (1,)), ((), ())), preferred_element_type=F32)


def _dot_tn(a, b):
    return lax.dot_general(a, b, (((0,), (0,)), ((), ())), preferred_element_type=F32)


def _split(x, n):
    parts, r = [], x
    for i in range(n):
        p = r.astype(BF16)
        parts.append(p)
        if i + 1 < n:
            r = r - p.astype(F32)
    return parts


def _xdot(x, m01, n=3):
    return sum(_dot(p, m01) for p in _split(x, n))


def _xdot_l(m01, x, n=3):
    return sum(_dot(m01, p) for p in _split(x, n))


def _xdot_tn(x, m01, n=3):
    return sum(_dot_tn(p, m01) for p in _split(x, n))


def _iota(shape, dim):
    return lax.broadcasted_iota(jnp.int32, shape, dim)


def _one_hot(cond):
    return jnp.where(cond, 1.0, 0.0).astype(BF16)


def _sigmoid(x):
    return 1.0 / (1.0 + jnp.exp(-x))


def _softplus(x):
    return jnp.maximum(x, 0.0) + jnp.log(1.0 + jnp.exp(-jnp.abs(x)))


def _rows8(x):
    r = _iota((SUBLANES, x.shape[-1]), 0)
    return jnp.where(r == 0, jnp.broadcast_to(x, (SUBLANES, x.shape[-1])), 0.0)


def _col_bcast(row, n):
    out = jnp.broadcast_to(row, (LANES, row.shape[-1])).T
    if n == LANES:
        return out
    return jnp.concatenate([out] * (n // LANES), axis=1)


def _rmsnorm_kernel(x_ref, g_ref, o_ref):
    x = x_ref[...]
    ms = jnp.mean(x * x, axis=-1, keepdims=True)
    o_ref[...] = ((x * lax.rsqrt(ms + NORM_EPS)) * g_ref[...]).astype(o_ref.dtype)


def _rmsnorm(x, g, out_dtype):
    m, d = x.shape
    tr = min(m, 256)
    return pl.pallas_call(
        _rmsnorm_kernel,
        grid=(m // tr,),
        in_specs=[pl.BlockSpec((tr, d), lambda i: (i, 0)), pl.BlockSpec((1, d), lambda i: (0, 0))],
        out_specs=pl.BlockSpec((tr, d), lambda i: (i, 0)),
        out_shape=jax.ShapeDtypeStruct((m, d), out_dtype),
        compiler_params=_params("parallel"),
        name="rmsnorm",
    )(x, g.reshape(1, d))


def _gelu_tanh(x):
    return x * (0.5 * (1.0 + jnp.tanh(math.sqrt(2.0 / math.pi) * (x + 0.044715 * (x * x * x)))))


def _mm_kernel(*refs, nk, b_f32, has_bias, act, has_res, n_out):
    it = iter(refs)
    a_ref, a2_ref, b_ref = next(it), next(it), next(it)
    bias_ref = next(it) if has_bias else None
    res_ref = next(it) if has_res else None
    res2_ref = next(it) if has_res else None
    o_refs = [next(it) for _ in range(n_out)]
    o2_ref = next(it)
    bbf_ref = next(it) if b_f32 else None
    acc_ref = next(it) if nk > 1 else None
    acc2_ref = next(it) if nk > 1 else None
    i = pl.program_id(1)
    k = pl.program_id(2)

    def finish(r, res):
        if has_bias:
            r = r + bias_ref[...]
        if act == "gelu":
            r = _gelu_tanh(r)
        elif act == "log_sigmoid":
            r = -_softplus(-r)
        if has_res:
            r = res[...] + r
        return r

    def rhs():
        return bbf_ref[...] if b_f32 else b_ref[...]

    if nk == 1:
        @pl.when(i == 0)
        def _():
            if b_f32:
                bbf_ref[...] = b_ref[...].astype(BF16)
            o2_ref[...] = finish(_dot(a2_ref[...], rhs()), res2_ref)

        r = finish(_dot(a_ref[...], rhs()), res_ref)
        for o in o_refs:
            o[...] = r.astype(o.dtype)
    else:
        if b_f32:
            bbf_ref[...] = b_ref[...].astype(BF16)
        part = _dot(a_ref[...], rhs())

        @pl.when(k == 0)
        def _():
            acc_ref[...] = part

        @pl.when(k > 0)
        def _():
            acc_ref[...] += part

        @pl.when(i == 0)
        def _():
            part2 = _dot(a2_ref[...], rhs())

            @pl.when(k == 0)
            def _():
                acc2_ref[...] = part2

            @pl.when(k > 0)
            def _():
                acc2_ref[...] += part2

        @pl.when(k == nk - 1)
        def _():
            r = finish(acc_ref[...], res_ref)
            for o in o_refs:
                o[...] = r.astype(o.dtype)

            @pl.when(i == 0)
            def _():
                o2_ref[...] = finish(acc2_ref[...], res2_ref)


def _mm_tiles(m, k, n, b_bytes, n_out_bytes, has_res):
    tm = min(m, 1024)
    tn = next((t for t in (512, 256, 128) if n % t == 0), n)
    for parts in range(1, 64):
        if k % parts or (k // parts) % LANES:
            continue
        tk = k // parts
        use = 2 * tm * tk * 2 + 2 * tk * tn * b_bytes + (tk * tn * 2 if b_bytes == 4 else 0)
        use += 2 * tm * tn * n_out_bytes + (2 * tm * tn * 4 if has_res else 0)
        use += tm * tn * 4 if parts > 1 else 0
        if use <= MM_BUDGET:
            return tm, tn, tk
    raise ValueError("no matmul tiling fits VMEM")


def _matmul(a, a2, w, *, col0=0, ncols=None, bias=None, act=None, res=None, res2=None,
            out_dtypes=(F32,)):
    m, kdim = a.shape
    ncols = w.shape[1] if ncols is None else ncols
    b_f32 = w.dtype == F32
    has_bias, has_res = bias is not None, res is not None
    ob = sum(jnp.dtype(d).itemsize for d in out_dtypes)
    tm, tn, tk = _mm_tiles(m, kdim, ncols, 4 if b_f32 else 2, ob, has_res)
    assert col0 % tn == 0 and ncols % tn == 0 and m % tm == 0 and a2.shape == (SUBLANES, kdim)
    nj, ni, nk = ncols // tn, m // tm, kdim // tk
    j0 = col0 // tn

    in_specs = [
        pl.BlockSpec((tm, tk), lambda j, i, k: (i, k)),
        pl.BlockSpec((SUBLANES, tk), lambda j, i, k: (0, k)),
        pl.BlockSpec((tk, tn), lambda j, i, k: (k, j + j0)),
    ]
    args = [a, a2, w]
    if has_bias:
        in_specs.append(pl.BlockSpec((1, tn), lambda j, i, k: (0, j + j0)))
        args.append(bias.reshape(1, -1))
    if has_res:
        in_specs.append(pl.BlockSpec((tm, tn), lambda j, i, k: (i, j)))
        in_specs.append(pl.BlockSpec((SUBLANES, tn), lambda j, i, k: (0, j)))
        args += [res, res2]
    out_specs = [pl.BlockSpec((tm, tn), lambda j, i, k: (i, j)) for _ in out_dtypes]
    out_specs.append(pl.BlockSpec((SUBLANES, tn), lambda j, i, k: (0, j)))
    out_shape = [jax.ShapeDtypeStruct((m, ncols), d) for d in out_dtypes]
    out_shape.append(jax.ShapeDtypeStruct((SUBLANES, ncols), F32))
    scratch = []
    if b_f32:
        scratch.append(pltpu.VMEM((tk, tn), BF16))
    if nk > 1:
        scratch += [pltpu.VMEM((tm, tn), F32), pltpu.VMEM((SUBLANES, tn), F32)]
    outs = pl.pallas_call(
        functools.partial(_mm_kernel, nk=nk, b_f32=b_f32, has_bias=has_bias, act=act,
                          has_res=has_res, n_out=len(out_dtypes)),
        grid=(nj, ni, nk),
        in_specs=in_specs,
        out_specs=out_specs,
        out_shape=out_shape,
        scratch_shapes=scratch,
        compiler_params=_params("arbitrary", "arbitrary", "arbitrary"),
        name="proj_matmul",
    )(*args)
    return list(outs[:-1]), outs[-1]


def _shift_rows(x, prev8, k):
    xs = pltpu.roll(x, k, 0)
    pk = pltpu.roll(prev8, k, 0)
    row = _iota((SUBLANES, x.shape[1]), 0)
    top = jnp.where(row < k, pk, xs[:SUBLANES])
    return jnp.concatenate([top, xs[SUBLANES:]], axis=0)


def _causal_conv(x, prev8, w_ref, b_ref):
    width = w_ref.shape[0]
    acc = x * w_ref[width - 1:width, :] + b_ref[...]
    for k in range(1, width):
        acc = acc + _shift_rows(x, prev8, k) * w_ref[width - 1 - k:width - k, :]
    return acc


def _conv_rows(cur, prev_ref, w_ref, b_ref):
    width = w_ref.shape[0]
    acc = cur * w_ref[width - 1:width, :] + b_ref[...]
    for k in range(width - 1):
        acc = acc + prev_ref[k] * w_ref[k:k + 1, :]
    return acc


def _ssd_conv_kernel(x_ref, p_ref, w_ref, b_ref, o_ref, *, blocks_per_seq):
    r = pl.program_id(0)
    first = (r % blocks_per_seq) == 0
    prev8 = jnp.where(first, 0.0, p_ref[...])
    y = _causal_conv(x_ref[...], prev8, w_ref, b_ref)
    o_ref[...] = y * _sigmoid(y)


def _ssd_conv(zx, col0, conv_w, conv_b, seq):
    m = zx.shape[0]
    width, c = conv_w.shape
    tr, tc = min(seq, 512), 512
    assert seq % tr == 0 and c % tc == 0 and col0 % tc == 0
    j0 = col0 // tc
    return pl.pallas_call(
        functools.partial(_ssd_conv_kernel, blocks_per_seq=seq // tr),
        grid=(m // tr, c // tc),
        in_specs=[
            pl.BlockSpec((tr, tc), lambda r, j: (r, j + j0)),
            pl.BlockSpec((SUBLANES, tc), lambda r, j: (jnp.maximum(r * (tr // SUBLANES) - 1, 0), j + j0)),
            pl.BlockSpec((width, tc), lambda r, j: (0, j)),
            pl.BlockSpec((1, tc), lambda r, j: (0, j)),
        ],
        out_specs=pl.BlockSpec((tr, tc), lambda r, j: (r, j)),
        out_shape=jax.ShapeDtypeStruct((m, c), F32),
        compiler_params=_params("parallel", "parallel"),
        name="ssd_conv",
    )(zx, zx, conv_w, conv_b.reshape(1, c))


def _gated_group_norm(y, z, nw):
    g = y * (z * _sigmoid(z))
    ms = jnp.mean(g * g, axis=-1, keepdims=True)
    return (g * lax.rsqrt(ms + NORM_EPS)) * nw


def _ssd_scan_kernel(z_ref, x_ref, bm_ref, cm_ref, dt_ref, dtb_ref, alog_ref, dsk_ref, nw_ref,
                     g_ref, st_out_ref, st_ref, cumt_ref, dtt_ref, *, hpg, nchunk):
    g = pl.program_id(1)
    ci = pl.program_id(2)
    c = x_ref.shape[0]
    heads = dt_ref.shape[1]
    p_dim = x_ref.shape[1] // hpg

    @pl.when(ci == 0)
    def _():
        st_ref[...] = jnp.zeros_like(st_ref)

    dt = _softplus(dt_ref[...] + dtb_ref[...])
    dta = dt * (-jnp.exp(alog_ref[...]))
    tril = _one_hot(_iota((c, c), 0) >= _iota((c, c), 1))
    cum = _xdot_l(tril, dta)
    cumt_ref[...] = cum.T
    dtt_ref[...] = dt.T
    cum_last = cum[c - 1:c, :]
    gw = hpg * p_dim
    spread = _one_hot(_iota((heads, gw), 0) == g * hpg + _iota((heads, gw), 1) // p_dim)
    exp_cum = _xdot(jnp.exp(cum), spread)
    to_end = _xdot(jnp.exp(cum_last - cum) * dt, spread)
    dec_last = _xdot(_rows8(jnp.exp(cum_last)), spread)[0:1]

    x = x_ref[...]
    bm = bm_ref[...].astype(BF16)
    cm = cm_ref[...].astype(BF16)
    cb = _dot_nt(cm, bm)
    tri = _iota((c, c), 0) >= _iota((c, c), 1)
    heads_per_tile = LANES // p_dim
    lane = _iota((c, LANES), 1)
    tiles = []
    for tile in range(gw // LANES):
        xt = x[:, tile * LANES:(tile + 1) * LANES]
        acc = None
        for e in range(heads_per_tile):
            h = g * hpg + tile * heads_per_tile + e
            cs = cumt_ref[pl.ds(h, 1), :]
            ct = _col_bcast(cs, c)
            seg = ct - cs
            dec = jnp.where(tri, jnp.exp(jnp.where(tri, seg, 0.0)), 0.0)
            w = cb * (dec * dtt_ref[pl.ds(h, 1), :])
            xm = jnp.where((lane >= e * p_dim) & (lane < (e + 1) * p_dim), xt, 0.0)
            part = _dot(w.astype(BF16), xm.astype(BF16))
            acc = part if acc is None else acc + part
        tiles.append(acc)
    y = jnp.concatenate(tiles, axis=1) if len(tiles) > 1 else tiles[0]
    st = st_ref[...]
    y = y + _dot_nt(cm, st.astype(BF16)) * exp_cum + dsk_ref[...] * x
    g_ref[...] = _gated_group_norm(y, z_ref[...], nw_ref[...]).astype(g_ref.dtype)

    upd = _dot_tn((x * to_end).astype(BF16), bm)
    new = st * _col_bcast(dec_last, st.shape[1]) + upd
    st_ref[...] = new

    @pl.when(ci == nchunk - 1)
    def _():
        st_out_ref[0, 0] = new


def _ssd_scan(zx, xc, dt_raw, dt_bias, a_log, dskip_x, norm_w, *, batch, seq, groups, hpg, p_dim, n_state):
    m = zx.shape[0]
    c = math.gcd(seq, SSD_CHUNK)
    assert c == SSD_CHUNK, "sequence length must be a multiple of the SSD chunk"
    nchunk = seq // c
    gw = hpg * p_dim
    heads = groups * hpg
    d_inner = heads * p_dim
    assert gw % LANES == 0 and LANES % p_dim == 0 and n_state == LANES and heads == LANES
    nxb = d_inner // n_state
    row = lambda b, g, ci: b * nchunk + ci
    return pl.pallas_call(
        functools.partial(_ssd_scan_kernel, hpg=hpg, nchunk=nchunk),
        grid=(batch, groups, nchunk),
        in_specs=[
            pl.BlockSpec((c, gw), lambda b, g, ci: (row(b, g, ci), g)),
            pl.BlockSpec((c, gw), lambda b, g, ci: (row(b, g, ci), g)),
            pl.BlockSpec((c, n_state), lambda b, g, ci: (row(b, g, ci), nxb + g)),
            pl.BlockSpec((c, n_state), lambda b, g, ci: (row(b, g, ci), nxb + groups + g)),
            pl.BlockSpec((c, heads), lambda b, g, ci: (row(b, g, ci), 0)),
            pl.BlockSpec((1, heads), lambda b, g, ci: (0, 0)),
            pl.BlockSpec((1, heads), lambda b, g, ci: (0, 0)),
            pl.BlockSpec((1, gw), lambda b, g, ci: (0, g)),
            pl.BlockSpec((1, gw), lambda b, g, ci: (0, g)),
        ],
        out_specs=[
            pl.BlockSpec((c, gw), lambda b, g, ci: (row(b, g, ci), g)),
            pl.BlockSpec((1, 1, gw, n_state), lambda b, g, ci: (b, g, 0, 0)),
        ],
        out_shape=[
            jax.ShapeDtypeStruct((m, d_inner), BF16),
            jax.ShapeDtypeStruct((batch, groups, gw, n_state), F32),
        ],
        scratch_shapes=[pltpu.VMEM((gw, n_state), F32), pltpu.VMEM((heads, c), F32), pltpu.VMEM((heads, c), F32)],
        compiler_params=_params("parallel", "parallel", "arbitrary"),
        name="ssd_scan",
    )(zx, xc, xc, xc, dt_raw, dt_bias.reshape(1, -1), a_log.reshape(1, -1), dskip_x.reshape(1, -1),
      norm_w.reshape(1, -1))


def _ssd_conv_sample_kernel(x_ref, p_ref, w_ref, b_ref, o_ref):
    y = _conv_rows(x_ref[...], p_ref, w_ref, b_ref)
    o_ref[...] = y * _sigmoid(y)


def _ssd_conv_sample(proj, col0, prev_t, conv_w, conv_b):
    width, c = conv_w.shape
    tc = 512
    j0 = col0 // tc
    return pl.pallas_call(
        _ssd_conv_sample_kernel,
        grid=(c // tc,),
        in_specs=[
            pl.BlockSpec((SUBLANES, tc), lambda j: (0, j + j0)),
            pl.BlockSpec((width - 1, SUBLANES, tc), lambda j: (0, 0, j)),
            pl.BlockSpec((width, tc), lambda j: (0, j)),
            pl.BlockSpec((1, tc), lambda j: (0, j)),
        ],
        out_specs=pl.BlockSpec((SUBLANES, tc), lambda j: (0, j)),
        out_shape=jax.ShapeDtypeStruct((SUBLANES, c), F32),
        compiler_params=_params("parallel"),
        name="ssd_conv_sample",
    )(proj, prev_t, conv_w, conv_b.reshape(1, c))


def _ssd_step_kernel(z_ref, x_ref, bm_ref, cm_ref, dt_ref, dtb_ref, alog_ref, dsk_ref, nw_ref, st_ref,
                     g_ref, st_out_ref, *, hpg):
    g = pl.program_id(1)
    heads = dt_ref.shape[2]
    gw = x_ref.shape[2]
    p_dim = gw // hpg
    dt = _softplus(dt_ref[0] + dtb_ref[...])
    dta = dt * (-jnp.exp(alog_ref[...]))
    spread = _one_hot(_iota((heads, gw), 0) == g * hpg + _iota((heads, gw), 1) // p_dim)
    dec = _xdot(_rows8(jnp.exp(dta)), spread)[0:1]
    dtx = _xdot(_rows8(dt), spread)[0:1]
    x = x_ref[0]
    bm = bm_ref[0]
    cm = cm_ref[0]
    st = st_ref[0, 0]
    n_state = st.shape[1]
    cb = jnp.sum(cm * bm, axis=-1, keepdims=True)
    y_state = _dot_nt(jnp.broadcast_to(cm, (SUBLANES, n_state)).astype(BF16), st.astype(BF16))[0:1]
    y = cb * dtx * x + y_state * dec + dsk_ref[...] * x
    g_ref[0] = _gated_group_norm(y, z_ref[0], nw_ref[...]).astype(g_ref.dtype)
    st_out_ref[0, 0] = st * _col_bcast(dec, n_state) + _col_bcast(dtx * x, n_state) * bm


def _ssd_step(proj3, xc3, st, dt_bias, a_log, dskip_x, norm_w, *, groups, hpg, p_dim, n_state):
    nb = proj3.shape[0]
    gw = hpg * p_dim
    heads = groups * hpg
    d_inner = heads * p_dim
    nxb = d_inner // n_state
    dt_blk = (2 * d_inner + 2 * groups * n_state) // heads
    assert (2 * d_inner + 2 * groups * n_state) % heads == 0
    return pl.pallas_call(
        functools.partial(_ssd_step_kernel, hpg=hpg),
        grid=(nb, groups),
        in_specs=[
            pl.BlockSpec((1, 1, gw), lambda b, g: (b, 0, g)),
            pl.BlockSpec((1, 1, gw), lambda b, g: (b, 0, g)),
            pl.BlockSpec((1, 1, n_state), lambda b, g: (b, 0, nxb + g)),
            pl.BlockSpec((1, 1, n_state), lambda b, g: (b, 0, nxb + groups + g)),
            pl.BlockSpec((1, 1, heads), lambda b, g: (b, 0, dt_blk)),
            pl.BlockSpec((1, heads), lambda b, g: (0, 0)),
            pl.BlockSpec((1, heads), lambda b, g: (0, 0)),
            pl.BlockSpec((1, gw), lambda b, g: (0, g)),
            pl.BlockSpec((1, gw), lambda b, g: (0, g)),
            pl.BlockSpec((1, 1, gw, n_state), lambda b, g: (b, g, 0, 0)),
        ],
        out_specs=[
            pl.BlockSpec((1, 1, gw), lambda b, g: (b, 0, g)),
            pl.BlockSpec((1, 1, gw, n_state), lambda b, g: (b, g, 0, 0)),
        ],
        out_shape=[
            jax.ShapeDtypeStruct((nb, 1, d_inner), BF16),
            jax.ShapeDtypeStruct((nb, groups, gw, n_state), F32),
        ],
        compiler_params=_params("parallel", "parallel"),
        name="ssd_step",
    )(proj3, xc3, xc3, xc3, proj3, dt_bias.reshape(1, -1), a_log.reshape(1, -1), dskip_x.reshape(1, -1),
      norm_w.reshape(1, -1), st)


def _ffn_gate_kernel(g_ref, v_ref, pg_ref, pv_ref, wg_ref, wv_ref, bg_ref, bv_ref, o_ref, *, blocks_per_seq):
    r = pl.program_id(0)
    first = (r % blocks_per_seq) == 0
    gate = _causal_conv(g_ref[...], jnp.where(first, 0.0, pg_ref[...]), wg_ref, bg_ref)
    val = _causal_conv(v_ref[...], jnp.where(first, 0.0, pv_ref[...]), wv_ref, bv_ref)
    o_ref[...] = ((gate * _sigmoid(gate)) * val).astype(o_ref.dtype)


def _ffn_gate(up, conv_w, conv_b, seq):
    m, two_f = up.shape
    f = two_f // 2
    width = conv_w.shape[0]
    tr = min(seq, 1024)
    tc = next(t for t in (512, 256, 128) if f % t == 0)
    nc = f // tc
    assert seq % tr == 0
    prev_idx = lambda r: jnp.maximum(r * (tr // SUBLANES) - 1, 0)
    b2 = conv_b.reshape(1, two_f)
    return pl.pallas_call(
        functools.partial(_ffn_gate_kernel, blocks_per_seq=seq // tr),
        grid=(m // tr, nc),
        in_specs=[
            pl.BlockSpec((tr, tc), lambda r, j: (r, j)),
            pl.BlockSpec((tr, tc), lambda r, j: (r, j + nc)),
            pl.BlockSpec((SUBLANES, tc), lambda r, j: (prev_idx(r), j)),
            pl.BlockSpec((SUBLANES, tc), lambda r, j: (prev_idx(r), j + nc)),
            pl.BlockSpec((width, tc), lambda r, j: (0, j)),
            pl.BlockSpec((width, tc), lambda r, j: (0, j + nc)),
            pl.BlockSpec((1, tc), lambda r, j: (0, j)),
            pl.BlockSpec((1, tc), lambda r, j: (0, j + nc)),
        ],
        out_specs=pl.BlockSpec((tr, tc), lambda r, j: (r, j)),
        out_shape=jax.ShapeDtypeStruct((m, f), BF16),
        compiler_params=_params("parallel", "parallel"),
        name="ffn_gate",
    )(up, up, up, up, conv_w, conv_w, b2, b2)


def _ffn_gate_sample_kernel(g_ref, v_ref, pg_ref, pv_ref, wg_ref, wv_ref, bg_ref, bv_ref, o_ref):
    gate = _conv_rows(g_ref[...], pg_ref, wg_ref, bg_ref)
    val = _conv_rows(v_ref[...], pv_ref, wv_ref, bv_ref)
    o_ref[...] = ((gate * _sigmoid(gate)) * val).astype(o_ref.dtype)


def _ffn_gate_sample(up, prev_t, conv_w, conv_b):
    two_f = up.shape[1]
    f = two_f // 2
    width = conv_w.shape[0]
    tc = next(t for t in (512, 256, 128) if f % t == 0)
    nc = f // tc
    b2 = conv_b.reshape(1, two_f)
    return pl.pallas_call(
        _ffn_gate_sample_kernel,
        grid=(nc,),
        in_specs=[
            pl.BlockSpec((SUBLANES, tc), lambda j: (0, j)),
            pl.BlockSpec((SUBLANES, tc), lambda j: (0, j + nc)),
            pl.BlockSpec((width - 1, SUBLANES, tc), lambda j: (0, 0, j)),
            pl.BlockSpec((width - 1, SUBLANES, tc), lambda j: (0, 0, j + nc)),
            pl.BlockSpec((width, tc), lambda j: (0, j)),
            pl.BlockSpec((width, tc), lambda j: (0, j + nc)),
            pl.BlockSpec((1, tc), lambda j: (0, j)),
            pl.BlockSpec((1, tc), lambda j: (0, j + nc)),
        ],
        out_specs=pl.BlockSpec((SUBLANES, tc), lambda j: (0, j)),
        out_shape=jax.ShapeDtypeStruct((SUBLANES, f), BF16),
        compiler_params=_params("parallel"),
        name="ffn_gate_sample",
    )(up, up, prev_t, prev_t, conv_w, conv_w, b2, b2)


def _flash_kernel(*refs, mode, hb, dh, tq, scale, nblk):
    if mode == "fox":
        q_ref, k_ref, v_ref, cum_ref, o_ref, m_ref, l_ref, acc_ref, cq_ref = refs
    else:
        q_ref, k_ref, v_ref, tt_ref, km_ref, o_ref, m_ref, l_ref, acc_ref, sel_ref = refs
    qi = pl.program_id(2)
    row = _iota((tq, tq), 0)
    col = _iota((tq, tq), 1)
    causal_f = jnp.where(row >= col, 1.0, 0.0)

    for hh in range(hb):
        m_ref[hh] = jnp.full((tq, 1), NEG_INF, F32)
        l_ref[hh] = jnp.zeros((tq, 1), F32)
        acc_ref[hh] = jnp.zeros((tq, dh), F32)
        q = q_ref[:, hh * dh:(hh + 1) * dh]
        if mode == "fox":
            cq = cum_ref[hh, :, pl.ds(pl.multiple_of(qi * tq, tq), tq)]
            cq_ref[hh] = jnp.broadcast_to(cq, (tq, tq)).T
        else:
            gate = sum(_dot_nt(q, part) for part in _split(km_ref[hh], 3))
            lane = _iota((tq, LANES), 1)
            valid = lane < qi
            gate = jnp.where(valid, gate, NEG_INF)
            rank = jnp.zeros((tq, LANES), F32)
            for mth in range(nblk):
                gm = gate[:, mth:mth + 1]
                beats = (gm > gate) | ((gm == gate) & (mth < lane))
                rank = rank + jnp.where(beats, 1.0, 0.0)
            keep = valid & (rank < float(min(MOBA_TOPK, nblk)))
            sel_ref[hh] = jnp.where(keep, 1.0, 0.0).astype(BF16)

    def body(j, carry):
        ki = qi - j
        ks = pl.multiple_of(ki * tq, tq)
        for hh in range(hb):
            q = q_ref[:, hh * dh:(hh + 1) * dh]
            k = k_ref[pl.ds(ks, tq), hh * dh:(hh + 1) * dh]
            v = v_ref[pl.ds(ks, tq), hh * dh:(hh + 1) * dh]
            s = _dot_nt(q, k) * scale
            if mode == "fox":
                s = s + cq_ref[hh] - cum_ref[hh, :, pl.ds(ks, tq)]
                keep = jnp.where(j > 0, 1.0, causal_f)
            else:
                s = s + tt_ref[hh, j]
                pick = _one_hot(_iota((LANES, tq), 0) == ki)
                keep = jnp.where(j > 0, _dot(sel_ref[hh], pick), causal_f)
            s = jnp.where(keep > 0.5, s, NEG_INF)
            m_old = m_ref[hh]
            m_new = jnp.maximum(m_old, jnp.max(s, axis=-1, keepdims=True))
            alpha = jnp.exp(m_old - m_new)
            p = jnp.exp(s - m_new)
            l_ref[hh] = alpha * l_ref[hh] + jnp.sum(p, axis=-1, keepdims=True)
            acc_ref[hh] = alpha * acc_ref[hh] + _dot(p.astype(BF16), v)
            m_ref[hh] = m_new
        return carry

    lax.fori_loop(0, qi + 1, body, 0)
    for hh in range(hb):
        o_ref[:, hh * dh:(hh + 1) * dh] = (acc_ref[hh] / l_ref[hh]).astype(o_ref.dtype)


def _flash(q, k, v, *, mode, batch, seq, heads, dh, scale, cum_t=None, tt=None, kmean=None):
    tq = MOBA_BLOCK
    hb = 4 if heads % 4 == 0 else (2 if heads % 2 == 0 else 1)
    assert seq % tq == 0 and dh == LANES
    nq = seq // tq
    ngrp = heads // hb
    in_specs = [
        pl.BlockSpec((tq, hb * dh), lambda b, hg, qi: (b * nq + qi, hg)),
        pl.BlockSpec((seq, hb * dh), lambda b, hg, qi: (b, hg)),
        pl.BlockSpec((seq, hb * dh), lambda b, hg, qi: (b, hg)),
    ]
    args = [q, k, v]
    scratch = [pltpu.VMEM((hb, tq, 1), F32), pltpu.VMEM((hb, tq, 1), F32), pltpu.VMEM((hb, tq, dh), F32)]
    if mode == "fox":
        in_specs.append(pl.BlockSpec((hb, 1, seq), lambda b, hg, qi: (b * ngrp + hg, 0, 0)))
        args.append(cum_t)
        scratch.append(pltpu.VMEM((hb, tq, tq), F32))
    else:
        nd = tt.shape[1]
        in_specs.append(pl.BlockSpec((hb, nd, tq, tq), lambda b, hg, qi: (hg, 0, 0, 0)))
        in_specs.append(pl.BlockSpec((hb, LANES, dh), lambda b, hg, qi: (b * ngrp + hg, 0, 0)))
        args += [tt, kmean]
        scratch.append(pltpu.VMEM((hb, tq, LANES), BF16))
    return pl.pallas_call(
        functools.partial(_flash_kernel, mode=mode, hb=hb, dh=dh, tq=tq, scale=scale, nblk=seq // MOBA_BLOCK),
        grid=(batch, ngrp, nq),
        in_specs=in_specs,
        out_specs=pl.BlockSpec((tq, hb * dh), lambda b, hg, qi: (b * nq + qi, hg)),
        out_shape=jax.ShapeDtypeStruct(q.shape, BF16),
        scratch_shapes=scratch,
        compiler_params=_params("parallel", "parallel", "arbitrary"),
        name="flash_" + mode,
    )(*args)


def _fox_cumsum_kernel(lf_ref, cum_ref, *, seq):
    parts = _split(lf_ref[...], 3)
    tc = min(seq, 512)
    for j in range(seq // tc):
        upto = _one_hot(_iota((seq, tc), 0) <= _iota((seq, tc), 1) + j * tc)
        cum_ref[:, 0, j * tc:(j + 1) * tc] = sum(_dot_tn(p, upto) for p in parts)


def _fox_cumsum(lf, batch, seq):
    heads = lf.shape[1]
    return pl.pallas_call(
        functools.partial(_fox_cumsum_kernel, seq=seq),
        grid=(batch,),
        in_specs=[pl.BlockSpec((seq, heads), lambda b: (b, 0))],
        out_specs=pl.BlockSpec((heads, 1, seq), lambda b: (b, 0, 0)),
        out_shape=jax.ShapeDtypeStruct((batch * heads, 1, seq), F32),
        compiler_params=_params("parallel"),
        name="fox_cumsum",
    )(lf)


def _t5_bucket(rel, n_buckets):
    exact = n_buckets // 2
    n = jnp.maximum(rel, 0)
    nf = jnp.maximum(n, 1).astype(F32)
    large = exact + (jnp.log(nf / exact) / math.log(REL_MAX_DIST / exact) * (n_buckets - exact)).astype(jnp.int32)
    return jnp.where(n < exact, n, jnp.minimum(large, n_buckets - 1))


def _rel_tile_kernel(rb_ref, o_ref, *, n_buckets, tq):
    h = pl.program_id(0)
    d = pl.program_id(1)
    rel = d * tq + _iota((tq, tq), 0) - _iota((tq, tq), 1)
    bucket = _t5_bucket(rel, n_buckets)
    out = jnp.zeros((tq, tq), F32)
    for j in range(n_buckets):
        out = jnp.where(bucket == j, rb_ref[h * n_buckets + j], out)
    o_ref[0, 0] = out


def _rel_tiles(rel_bias, nd):
    n_buckets, heads = rel_bias.shape
    tq = MOBA_BLOCK
    return pl.pallas_call(
        functools.partial(_rel_tile_kernel, n_buckets=n_buckets, tq=tq),
        grid=(heads, nd),
        in_specs=[pl.BlockSpec(memory_space=pltpu.SMEM)],
        out_specs=pl.BlockSpec((1, 1, tq, tq), lambda h, d: (h, d, 0, 0)),
        out_shape=jax.ShapeDtypeStruct((heads, nd, tq, tq), F32),
        compiler_params=_params("parallel", "parallel"),
        name="t5_bias_tiles",
    )(rel_bias.T.reshape(-1))


def _kmean_kernel(k_ref, o_ref, *, nblk, hb, dh):
    blk = MOBA_BLOCK
    rows = _iota((LANES, hb * dh), 0)
    out = jnp.zeros((LANES, hb * dh), F32)
    for n in range(nblk):
        mean = jnp.sum(k_ref[n * blk:(n + 1) * blk, :], axis=0, keepdims=True) * (1.0 / blk)
        out = jnp.where(rows == n, mean, out)
    for hh in range(hb):
        o_ref[hh] = out[:, hh * dh:(hh + 1) * dh]


def _kmean(k, batch, seq, heads, dh):
    nblk = seq // MOBA_BLOCK
    hb = 4 if heads % 4 == 0 else (2 if heads % 2 == 0 else 1)
    ngrp = heads // hb
    assert nblk <= LANES
    return pl.pallas_call(
        functools.partial(_kmean_kernel, nblk=nblk, hb=hb, dh=dh),
        grid=(batch, ngrp),
        in_specs=[pl.BlockSpec((seq, hb * dh), lambda b, hg: (b, hg))],
        out_specs=pl.BlockSpec((hb, LANES, dh), lambda b, hg: (b * ngrp + hg, 0, 0)),
        out_shape=jax.ShapeDtypeStruct((batch * heads, LANES, dh), F32),
        compiler_params=_params("parallel", "parallel"),
        name="moba_kmean",
    )(k)


def _lanes_to_sublanes(x3):
    t, _, h = x3.shape
    eye = _iota((t, h, h), 1) == _iota((t, h, h), 2)
    return jnp.sum(jnp.where(eye, jnp.broadcast_to(x3, (t, h, h)), 0.0), axis=-1, keepdims=True)


def _decode_kernel(*refs, mode, npages, scale):
    if mode == "fox":
        pt_ref, q_ref, kn_ref, vn_ref, k_ref, v_ref, lf_ref, lfn_ref, o_ref, m_ref, l_ref, acc_ref, carry_ref = refs
    else:
        pt_ref, q_ref, kn_ref, vn_ref, k_ref, v_ref, tbl_ref, sel_ref, rb_ref, o_ref, m_ref, l_ref, acc_ref = refs
    p = pl.program_id(1)
    q = q_ref[...]

    @pl.when(p == 0)
    def _():
        s0 = jnp.sum(q * kn_ref[...], axis=-1, keepdims=True) * scale
        if mode == "fox":
            carry_ref[...] = lfn_ref[0]
        else:
            s0 = s0 + _lanes_to_sublanes(rb_ref[0:1, :][None])
        m_ref[...] = s0
        l_ref[...] = jnp.ones_like(s0)
        acc_ref[...] = vn_ref[...]

    s = jnp.sum(k_ref[0] * q, axis=-1, keepdims=True) * scale
    ps = s.shape[0]
    if mode == "fox":
        lf = lf_ref[0]
        run = carry_ref[...]
        rows = [None] * ps
        for t in reversed(range(ps)):
            rows[t] = run
            run = run + lf[t:t + 1, :]
        carry_ref[...] = run
    else:
        tbl = tbl_ref[0]
        rows = [tbl[t:t + 1, :] for t in range(ps)]
    s = s + _lanes_to_sublanes(jnp.concatenate([r[None] for r in rows], axis=0))
    if mode == "moba":
        s = jnp.where(sel_ref[0] > 0.5, s, NEG_INF)
    m_old = m_ref[...]
    m_new = jnp.maximum(m_old, jnp.max(s, axis=0, keepdims=True))
    alpha = jnp.exp(m_old - m_new)
    pm = jnp.exp(s - m_new)
    l_new = alpha * l_ref[...] + jnp.sum(pm, axis=0, keepdims=True)
    acc = alpha * acc_ref[...] + jnp.sum(pm * v_ref[0], axis=0, keepdims=True)
    m_ref[...] = m_new
    l_ref[...] = l_new
    acc_ref[...] = acc

    @pl.when(p == npages - 1)
    def _():
        o_ref[...] = (acc / l_new).astype(o_ref.dtype)


def _decode_attn(mode, q, k_new, v_new, k_pool, v_pool, page_table, *, scale,
                 lf_pool=None, lf_new=None, bias_tbl=None, sel=None, rel_bias=None):
    nb, npages = page_table.shape
    _, ps, heads, dh = k_pool.shape
    rev = lambda p: npages - 1 - p
    page = lambda b, p, pt: pt[b * npages + rev(p)]
    tok = pl.BlockSpec((1, heads, dh), lambda b, p, pt: (b, 0, 0))
    pages = pl.BlockSpec((1, ps, heads, dh), lambda b, p, pt: (page(b, p, pt), 0, 0, 0))
    in_specs = [tok, tok, tok, pages, pages]
    args = [q, k_new, v_new, k_pool, v_pool]
    scratch = [pltpu.VMEM((1, heads, 1), F32), pltpu.VMEM((1, heads, 1), F32), pltpu.VMEM((1, heads, dh), F32)]
    if mode == "fox":
        in_specs.append(pl.BlockSpec((1, ps, heads), lambda b, p, pt: (page(b, p, pt), 0, 0)))
        in_specs.append(pl.BlockSpec((1, 1, heads), lambda b, p, pt: (b, 0, 0)))
        args += [lf_pool, lf_new.reshape(nb, 1, heads)]
        scratch.append(pltpu.VMEM((1, heads), F32))
    else:
        ppb = MOBA_BLOCK // ps
        in_specs.append(pl.BlockSpec((1, ps, heads), lambda b, p, pt: (rev(p), 0, 0)))
        in_specs.append(pl.BlockSpec((1, 1, heads, 1), lambda b, p, pt: (b, rev(p) // ppb, 0, 0)))
        in_specs.append(pl.BlockSpec(rel_bias.shape, lambda b, p, pt: (0, 0)))
        args += [bias_tbl, sel, rel_bias]
    return pl.pallas_call(
        functools.partial(_decode_kernel, mode=mode, npages=npages, scale=scale),
        grid_spec=pltpu.PrefetchScalarGridSpec(
            num_scalar_prefetch=1,
            grid=(nb, npages),
            in_specs=in_specs,
            out_specs=pl.BlockSpec((1, heads, dh), lambda b, p, pt: (b, 0, 0)),
            scratch_shapes=scratch,
        ),
        out_shape=jax.ShapeDtypeStruct((nb, heads, dh), BF16),
        compiler_params=_params("parallel", "arbitrary"),
        name="decode_" + mode,
    )(page_table.reshape(-1), *args)


def _moba_select_kernel(pt_ref, q_ref, k_ref, o_ref, ksum_ref, gate_ref, *, npages, ppb, nblk):
    p = pl.program_id(1)

    @pl.when(lax.rem(p, ppb) == 0)
    def _():
        ksum_ref[...] = jnp.zeros_like(ksum_ref)

    ksum_ref[...] += jnp.sum(k_ref[0], axis=0)

    @pl.when(lax.rem(p, ppb) == ppb - 1)
    def _():
        kmean = ksum_ref[...] * (1.0 / MOBA_BLOCK)
        gate_ref[pl.ds(lax.div(p, ppb), 1)] = jnp.sum(kmean * q_ref[0], axis=-1, keepdims=True)[None]

    @pl.when(p == npages - 1)
    def _():
        gate = gate_ref[...]
        blk = _iota(gate.shape, 0)
        rank = jnp.zeros(gate.shape, F32)
        for mth in range(nblk):
            gm = gate[mth:mth + 1]
            beats = (gm > gate) | ((gm == gate) & (mth < blk))
            rank = rank + jnp.where(beats, 1.0, 0.0)
        o_ref[0] = jnp.where(rank < float(min(MOBA_TOPK, nblk + 1)), 1.0, 0.0)


def _moba_select(q, k_pool, page_table):
    nb, npages = page_table.shape
    _, ps, heads, dh = k_pool.shape
    ppb = MOBA_BLOCK // ps
    nblk = npages // ppb
    assert MOBA_BLOCK % ps == 0 and npages % ppb == 0, "past length must be whole MoBA blocks"
    return pl.pallas_call(
        functools.partial(_moba_select_kernel, npages=npages, ppb=ppb, nblk=nblk),
        grid_spec=pltpu.PrefetchScalarGridSpec(
            num_scalar_prefetch=1,
            grid=(nb, npages),
            in_specs=[
                pl.BlockSpec((1, heads, dh), lambda b, p, pt: (b, 0, 0)),
                pl.BlockSpec((1, ps, heads, dh), lambda b, p, pt: (pt[b * npages + p], 0, 0, 0)),
            ],
            out_specs=pl.BlockSpec((1, nblk, heads, 1), lambda b, p, pt: (b, 0, 0, 0)),
            scratch_shapes=[pltpu.VMEM((heads, dh), F32), pltpu.VMEM((nblk, heads, 1), F32)],
        ),
        out_shape=jax.ShapeDtypeStruct((nb, nblk, heads, 1), F32),
        compiler_params=_params("parallel", "arbitrary"),
        name="moba_select",
    )(page_table.reshape(-1), q, k_pool)


def _rel_rows_kernel(rb_ref, o_ref, *, past, n_buckets):
    pg = pl.program_id(0)
    ps = o_ref.shape[1]
    kpos = pg * ps + _iota((ps, n_buckets), 0)
    bucket = _t5_bucket(past - kpos, n_buckets)
    hot = _one_hot(bucket == _iota((ps, n_buckets), 1))
    o_ref[0] = _xdot_l(hot, rb_ref[...])


def _rel_rows(rel_bias, npages, ps):
    n_buckets, heads = rel_bias.shape
    return pl.pallas_call(
        functools.partial(_rel_rows_kernel, past=npages * ps, n_buckets=n_buckets),
        grid=(npages,),
        in_specs=[pl.BlockSpec((n_buckets, heads), lambda p: (0, 0))],
        out_specs=pl.BlockSpec((1, ps, heads), lambda p: (p, 0, 0)),
        out_shape=jax.ShapeDtypeStruct((npages, ps, heads), F32),
        compiler_params=_params("parallel"),
        name="t5_bias_rows",
    )(rel_bias)


def _layernorm(v, g, b):
    mu = jnp.mean(v, axis=-1, keepdims=True)
    vc = v - mu
    var = jnp.mean(vc * vc, axis=-1, keepdims=True)
    return (vc * lax.rsqrt(var + NORM_EPS)) * g + b


def _gmlp_kernel(u_ref, v_ref, lg_ref, lb_ref, w_ref, bsp_ref, o_ref, *, ngroups):
    c = u_ref.shape[0]
    gw = u_ref.shape[1] // ngroups
    vn = _layernorm(v_ref[...], lg_ref[...], lb_ref[...])
    tri = _iota((c, c), 0) >= _iota((c, c), 1)
    for g in range(ngroups):
        w = jnp.where(tri, w_ref[g], 0.0).astype(BF16)
        s = _dot(w, vn[:, g * gw:(g + 1) * gw].astype(BF16)) + bsp_ref[g]
        o_ref[:, g * gw:(g + 1) * gw] = (u_ref[:, g * gw:(g + 1) * gw] * s).astype(o_ref.dtype)


def _gmlp_gate(z, ln_g, ln_b, w_sp, b_sp, seq):
    m = z.shape[0]
    half = z.shape[1] // 2
    ngroups, c, _ = w_sp.shape
    assert seq % c == 0
    return pl.pallas_call(
        functools.partial(_gmlp_kernel, ngroups=ngroups),
        grid=(m // c,),
        in_specs=[
            pl.BlockSpec((c, half), lambda i: (i, 0)),
            pl.BlockSpec((c, half), lambda i: (i, 1)),
            pl.BlockSpec((1, half), lambda i: (0, 0)),
            pl.BlockSpec((1, half), lambda i: (0, 0)),
            pl.BlockSpec((ngroups, c, c), lambda i: (0, 0, 0)),
            pl.BlockSpec((ngroups, c, 1), lambda i: (0, 0, 0)),
        ],
        out_specs=pl.BlockSpec((c, half), lambda i: (i, 0)),
        out_shape=jax.ShapeDtypeStruct((m, half), BF16),
        compiler_params=_params("parallel"),
        name="gmlp_gate",
    )(z, z, ln_g.reshape(1, half), ln_b.reshape(1, half), w_sp, b_sp.reshape(ngroups, c, 1))


def _gmlp_sample_kernel(u_ref, v_ref, lg_ref, lb_ref, w0_ref, b0_ref, o_ref, vn_ref):
    vn = _layernorm(v_ref[...], lg_ref[...], lb_ref[...])
    vn_ref[...] = vn
    o_ref[...] = (u_ref[...] * (vn * w0_ref[...] + b0_ref[...])).astype(o_ref.dtype)


def _gmlp_gate_sample(z, ln_g, ln_b, w0_x, b0_x):
    half = z.shape[1] // 2
    row = pl.BlockSpec((1, half), lambda i: (0, 0))
    return pl.pallas_call(
        _gmlp_sample_kernel,
        grid=(1,),
        in_specs=[pl.BlockSpec((SUBLANES, half), lambda i: (0, 0)), pl.BlockSpec((SUBLANES, half), lambda i: (0, 1)),
                  row, row, row, row],
        out_specs=[pl.BlockSpec((SUBLANES, half), lambda i: (0, 0)), pl.BlockSpec((SUBLANES, half), lambda i: (0, 0))],
        out_shape=[jax.ShapeDtypeStruct((SUBLANES, half), BF16), jax.ShapeDtypeStruct((SUBLANES, half), F32)],
        compiler_params=_params("arbitrary"),
        name="gmlp_gate_sample",
    )(z, z, ln_g.reshape(1, half), ln_b.reshape(1, half), w0_x.reshape(1, half), b0_x.reshape(1, half))


def kernel(x_prompt, x_sample, state_ssd_conv, state_ssd, cache_fox_k, cache_fox_v, cache_fox_logf, cache_moba_k, cache_moba_v, state_ffn_conv, page_table, norm_mix, norm_ffn, norm_final, ssd_w_in, ssd_conv_w, ssd_conv_b, ssd_dt_bias, ssd_a_log, ssd_d, ssd_norm, ssd_w_out, fox_w_qkv, fox_w_f, fox_b_f, fox_w_o, moba_w_qkv, moba_w_o, rel_bias, gmlp_w_in, gmlp_b_in, gmlp_ln_g, gmlp_ln_b, gmlp_w_sp, gmlp_b_sp, gmlp_w_out, ffn_w_up, ffn_conv_w, ffn_conv_b, ffn_w_down):
    batch, seq, d_model = x_prompt.shape
    nb, dec_seq, _ = x_sample.shape
    depth = norm_mix.shape[0]
    assert nb == SUBLANES and dec_seq == 1, "sample group must be 8 single-token sequences"
    m = batch * seq
    n_pages, page_size = page_table.shape[1], cache_fox_k.shape[2]
    past = n_pages * page_size

    _, _, ssd_heads, ssd_p, ssd_n = state_ssd.shape
    d_inner = ssd_heads * ssd_p
    conv_dim = ssd_conv_w.shape[2]
    ssd_groups = (conv_dim - d_inner) // (2 * ssd_n)
    hpg = ssd_heads // ssd_groups
    fox_heads, fox_dh = cache_fox_k.shape[3], cache_fox_k.shape[4]
    moba_heads, moba_dh = cache_moba_k.shape[3], cache_moba_k.shape[4]
    d_ff = ffn_w_down.shape[1]

    xp = x_prompt.reshape(m, d_model)
    xs = x_sample.reshape(nb, d_model)
    outs = {k: [] for k in ("p_ssd_conv", "p_ssd_state", "p_fox_k", "p_fox_v", "p_fox_lf", "p_moba_k", "p_moba_v",
                            "p_ffn", "s_ssd_conv", "s_ssd_state", "s_fox_k", "s_fox_v", "s_fox_lf", "s_moba_k",
                            "s_moba_v", "s_gmlp_v", "s_ffn")}

    def tail_rows(a, n):
        return a.reshape(batch, seq, a.shape[-1])[:, seq - n:]

    for i in range(depth):
        mixer, j = i % 4, i // 4
        hp = _rmsnorm(xp, norm_mix[i], BF16)
        hs = _rmsnorm(xs, norm_mix[i], BF16)
        if mixer == 0:
            n_zx = d_inner + conv_dim
            (zx,), zx_s = _matmul(hp, hs, ssd_w_in[j], col0=0, ncols=n_zx)
            (dt_raw,), dt_s = _matmul(hp, hs, ssd_w_in[j], col0=n_zx, ncols=ssd_heads)
            dskip_x = jnp.repeat(ssd_d[j], ssd_p)
            xc = _ssd_conv(zx, d_inner, ssd_conv_w[j], ssd_conv_b[j], seq)
            gp, st_p = _ssd_scan(zx, xc, dt_raw, ssd_dt_bias[j], ssd_a_log[j], dskip_x, ssd_norm[j], batch=batch,
                                 seq=seq, groups=ssd_groups, hpg=hpg, p_dim=ssd_p, n_state=ssd_n)
            outs["p_ssd_conv"].append(tail_rows(zx[:, d_inner:], ssd_conv_w.shape[1] - 1))
            outs["p_ssd_state"].append(st_p.reshape(batch, ssd_heads, ssd_p, ssd_n))
            prev = state_ssd_conv[j]
            xc_s = _ssd_conv_sample(zx_s, d_inner, prev.transpose(1, 0, 2), ssd_conv_w[j], ssd_conv_b[j])
            proj_s = jnp.concatenate([zx_s, dt_s], axis=1)
            gs, st_s = _ssd_step(proj_s.reshape(nb, 1, -1), xc_s.reshape(nb, 1, conv_dim),
                                 state_ssd[j].reshape(nb, ssd_groups, hpg * ssd_p, ssd_n), ssd_dt_bias[j],
                                 ssd_a_log[j], dskip_x, ssd_norm[j], groups=ssd_groups, hpg=hpg, p_dim=ssd_p,
                                 n_state=ssd_n)
            outs["s_ssd_conv"].append(jnp.concatenate([prev[:, 1:], zx_s[:, None, d_inner:]], axis=1))
            outs["s_ssd_state"].append(st_s.reshape(nb, ssd_heads, ssd_p, ssd_n))
            (xp,), xs = _matmul(gp, gs.reshape(nb, d_inner), ssd_w_out[j].astype(BF16), res=xp, res2=xs)
        elif mixer in (1, 2):
            fox = mixer == 1
            w_qkv, w_o = (fox_w_qkv[j], fox_w_o[j]) if fox else (moba_w_qkv[j], moba_w_o[j])
            heads, dh = (fox_heads, fox_dh) if fox else (moba_heads, moba_dh)
            hd = heads * dh
            scale = dh ** -0.5
            (q,), q_s = _matmul(hp, hs, w_qkv, col0=0, ncols=hd, out_dtypes=(BF16,))
            (k, kb), k_s = _matmul(hp, hs, w_qkv, col0=hd, ncols=hd, out_dtypes=(F32, BF16))
            (v, vb), v_s = _matmul(hp, hs, w_qkv, col0=2 * hd, ncols=hd, out_dtypes=(F32, BF16))
            pk = "fox" if fox else "moba"
            outs[f"p_{pk}_k"].append(k.reshape(batch, seq, heads, dh))
            outs[f"p_{pk}_v"].append(v.reshape(batch, seq, heads, dh))
            outs[f"s_{pk}_k"].append(k_s.reshape(nb, 1, heads, dh))
            outs[f"s_{pk}_v"].append(v_s.reshape(nb, 1, heads, dh))
            tok3 = [t.reshape(nb, heads, dh) for t in (q_s, k_s, v_s)]
            if fox:
                (lf,), lf_s = _matmul(hp, hs, fox_w_f[j], bias=fox_b_f[j], act="log_sigmoid")
                cum_t = _fox_cumsum(lf, batch, seq)
                outs["p_fox_lf"].append(lf.reshape(batch, seq, heads))
                outs["s_fox_lf"].append(lf_s.reshape(nb, 1, heads))
                op = _flash(q, kb, vb, mode="fox", batch=batch, seq=seq, heads=heads, dh=dh, scale=scale, cum_t=cum_t)
                os_ = _decode_attn("fox", *tok3, cache_fox_k[j], cache_fox_v[j], page_table, scale=scale,
                                   lf_pool=cache_fox_logf[j], lf_new=lf_s)
            else:
                tt = _rel_tiles(rel_bias, seq // MOBA_BLOCK)
                km = _kmean(k, batch, seq, heads, dh)
                op = _flash(q, kb, vb, mode="moba", batch=batch, seq=seq, heads=heads, dh=dh, scale=scale, tt=tt,
                            kmean=km)
                sel = _moba_select(tok3[0], cache_moba_k[j], page_table)
                os_ = _decode_attn("moba", *tok3, cache_moba_k[j], cache_moba_v[j], page_table, scale=scale,
                                   bias_tbl=_rel_rows(rel_bias, n_pages, page_size), sel=sel, rel_bias=rel_bias)
            (xp,), xs = _matmul(op, os_.reshape(nb, hd), w_o, res=xp, res2=xs)
        else:
            (z,), z_s = _matmul(hp, hs, gmlp_w_in[j], bias=gmlp_b_in[j], act="gelu")
            half = z.shape[1] // 2
            gp = _gmlp_gate(z, gmlp_ln_g[j], gmlp_ln_b[j], gmlp_w_sp[j], gmlp_b_sp[j], seq)
            gw = half // gmlp_w_sp.shape[1]
            gs, vn_s = _gmlp_gate_sample(z_s, gmlp_ln_g[j], gmlp_ln_b[j], jnp.repeat(gmlp_w_sp[j, :, 0, 0], gw),
                                         jnp.repeat(gmlp_b_sp[j, :, 0], gw))
            outs["s_gmlp_v"].append(vn_s.reshape(nb, 1, half))
            (xp,), xs = _matmul(gp, gs, gmlp_w_out[j].astype(BF16), res=xp, res2=xs)

        hp = _rmsnorm(xp, norm_ffn[i], BF16)
        hs = _rmsnorm(xs, norm_ffn[i], BF16)
        (up,), up_s = _matmul(hp, hs, ffn_w_up[i])
        act = _ffn_gate(up, ffn_conv_w[i], ffn_conv_b[i], seq)
        prev = state_ffn_conv[i]
        act_s = _ffn_gate_sample(up_s, prev.transpose(1, 0, 2), ffn_conv_w[i], ffn_conv_b[i])
        outs["p_ffn"].append(tail_rows(up, ffn_conv_w.shape[1] - 1))
        outs["s_ffn"].append(jnp.concatenate([prev[:, 1:], up_s[:, None]], axis=1))
        (xp,), xs = _matmul(act, act_s, ffn_w_down[i].astype(BF16), res=xp, res2=xs)

    y_prompt = _rmsnorm(xp, norm_final, F32).reshape(batch, seq, d_model)
    y_sample = _rmsnorm(xs, norm_final, F32).reshape(nb, 1, d_model)
    st = lambda key: jnp.stack(outs[key])
    return (y_prompt, y_sample,
            st("p_ssd_conv"), st("p_ssd_state"), st("p_fox_k"), st("p_fox_v"), st("p_fox_lf"), st("p_moba_k"),
            st("p_moba_v"), st("p_ffn"),
            st("s_ssd_conv"), st("s_ssd_state"), st("s_fox_k"), st("s_fox_v"), st("s_fox_lf"), st("s_moba_k"),
            st("s_moba_v"), st("s_gmlp_v"), st("s_ffn"))
```

```python
import functools
import math

import jax
import jax.numpy as jnp
from jax import lax
from jax.experimental import pallas as pl
from jax.experimental.pallas import tpu as pltpu

F32 = jnp.float32
BF16 = jnp.bfloat16

NORM_EPS = 1e-6
SSD_CHUNK = 128
MOBA_BLOCK = 256
MOBA_TOPK = 3
REL_MAX_DIST = 128
LANES = 128
SUBLANES = 8
VMEM_LIMIT = 56 * 1024 * 1024
MM_BUDGET = 46 * 1024 * 1024
NEG_INF = float("-inf")


def _params(*sem):
    return pltpu.CompilerParams(dimension_semantics=sem, vmem_limit_bytes=VMEM_LIMIT)


def _dot(a, b):
    return jnp.dot(a, b, preferred_element_type=F32)


def _dot_nt(a, b):
    return lax.dot_general(a, b, (((1,), (1,)), ((), ())), preferred_element_type=F32)


def _dot_tn(a, b):
    return lax.dot_general(a, b, (((0,), (0,)), ((), ())), preferred_element_type=F32)


def _split(x, n):
    parts, r = [], x
    for i in range(n):
        p = r.astype(BF16)
        parts.append(p)
        if i + 1 < n:
            r = r - p.astype(F32)
    return parts


def _xdot(x, m01, n=3):
    return sum(_dot(p, m01) for p in _split(x, n))


def _xdot_l(m01, x, n=3):
    return sum(_dot(m01, p) for p in _split(x, n))


def _xdot_tn(x, m01, n=3):
    return sum(_dot_tn(p, m01) for p in _split(x, n))


def _iota(shape, dim):
    return lax.broadcasted_iota(jnp.int32, shape, dim)


def _one_hot(cond):
    return jnp.where(cond, 1.0, 0.0).astype(BF16)


def _sigmoid(x):
    return 1.0 / (1.0 + jnp.exp(-x))


def _softplus(x):
    return jnp.maximum(x, 0.0) + jnp.log(1.0 + jnp.exp(-jnp.abs(x)))


def _rows8(x):
    r = _iota((SUBLANES, x.shape[-1]), 0)
    return jnp.where(r == 0, jnp.broadcast_to(x, (SUBLANES, x.shape[-1])), 0.0)


def _col_bcast(row, n):
    out = jnp.broadcast_to(row, (LANES, row.shape[-1])).T
    if n == LANES:
        return out
    return jnp.concatenate([out] * (n // LANES), axis=1)


def _rmsnorm_kernel(x_ref, g_ref, o_ref):
    x = x_ref[...]
    ms = jnp.mean(x * x, axis=-1, keepdims=True)
    o_ref[...] = ((x * lax.rsqrt(ms + NORM_EPS)) * g_ref[...]).astype(o_ref.dtype)


def _rmsnorm(x, g, out_dtype):
    m, d = x.shape
    tr = min(m, 256)
    return pl.pallas_call(
        _rmsnorm_kernel,
        grid=(m // tr,),
        in_specs=[pl.BlockSpec((tr, d), lambda i: (i, 0)), pl.BlockSpec((1, d), lambda i: (0, 0))],
        out_specs=pl.BlockSpec((tr, d), lambda i: (i, 0)),
        out_shape=jax.ShapeDtypeStruct((m, d), out_dtype),
        compiler_params=_params("parallel"),
        name="rmsnorm",
    )(x, g.reshape(1, d))


def _gelu_tanh(x):
    return x * (0.5 * (1.0 + jnp.tanh(math.sqrt(2.0 / math.pi) * (x + 0.044715 * (x * x * x)))))


def _mm_kernel(*refs, nk, b_f32, has_bias, act, has_res, n_out):
    it = iter(refs)
    a_ref, a2_ref, b_ref = next(it), next(it), next(it)
    bias_ref = next(it) if has_bias else None
    res_ref = next(it) if has_res else None
    res2_ref = next(it) if has_res else None
    o_refs = [next(it) for _ in range(n_out)]
    o2_ref = next(it)
    bbf_ref = next(it) if b_f32 else None
    acc_ref = next(it) if nk > 1 else None
    acc2_ref = next(it) if nk > 1 else None
    i = pl.program_id(1)
    k = pl.program_id(2)

    def finish(r, res):
        if has_bias:
            r = r + bias_ref[...]
        if act == "gelu":
            r = _gelu_tanh(r)
        elif act == "log_sigmoid":
            r = -_softplus(-r)
        if has_res:
            r = res[...] + r
        return r

    def rhs():
        return bbf_ref[...] if b_f32 else b_ref[...]

    if nk == 1:
        @pl.when(i == 0)
        def _():
            if b_f32:
                bbf_ref[...] = b_ref[...].astype(BF16)
            o2_ref[...] = finish(_dot(a2_ref[...], rhs()), res2_ref)

        r = finish(_dot(a_ref[...], rhs()), res_ref)
        for o in o_refs:
            o[...] = r.astype(o.dtype)
    else:
        if b_f32:
            bbf_ref[...] = b_ref[...].astype(BF16)
        part = _dot(a_ref[...], rhs())

        @pl.when(k == 0)
        def _():
            acc_ref[...] = part

        @pl.when(k > 0)
        def _():
            acc_ref[...] += part

        @pl.when(i == 0)
        def _():
            part2 = _dot(a2_ref[...], rhs())

            @pl.when(k == 0)
            def _():
                acc2_ref[...] = part2

            @pl.when(k > 0)
            def _():
                acc2_ref[...] += part2

        @pl.when(k == nk - 1)
        def _():
            r = finish(acc_ref[...], res_ref)
            for o in o_refs:
                o[...] = r.astype(o.dtype)

            @pl.when(i == 0)
            def _():
                o2_ref[...] = finish(acc2_ref[...], res2_ref)


def _mm_tiles(m, k, n, b_bytes, n_out_bytes, has_res):
    tm = min(m, 1024)
    tn = next((t for t in (512, 256, 128) if n % t == 0), n)
    for parts in range(1, 64):
        if k % parts or (k // parts) % LANES:
            continue
        tk = k // parts
        use = 2 * tm * tk * 2 + 2 * tk * tn * b_bytes + (tk * tn * 2 if b_bytes == 4 else 0)
        use += 2 * tm * tn * n_out_bytes + (2 * tm * tn * 4 if has_res else 0)
        use += tm * tn * 4 if parts > 1 else 0
        if use <= MM_BUDGET:
            return tm, tn, tk
    raise ValueError("no matmul tiling fits VMEM")


def _matmul(a, a2, w, *, layer=None, col0=0, ncols=None, bias=None, act=None, res=None, res2=None,
            out_dtypes=(F32,)):
    m, kdim = a.shape
    ncols = w.shape[-1] if ncols is None else ncols
    b_f32 = w.dtype == F32
    has_bias, has_res = bias is not None, res is not None
    ob = sum(jnp.dtype(d).itemsize for d in out_dtypes)
    tm, tn, tk = _mm_tiles(m, kdim, ncols, 4 if b_f32 else 2, ob, has_res)
    assert col0 % tn == 0 and ncols % tn == 0 and m % tm == 0 and a2.shape == (SUBLANES, kdim)
    nj, ni, nk = ncols // tn, m // tm, kdim // tk
    j0 = col0 // tn

    in_specs = [
        pl.BlockSpec((tm, tk), lambda j, i, k: (i, k)),
        pl.BlockSpec((SUBLANES, tk), lambda j, i, k: (0, k)),
        pl.BlockSpec((tk, tn), lambda j, i, k: (k, j + j0)) if layer is None else
        pl.BlockSpec((None, tk, tn), lambda j, i, k: (layer, k, j + j0)),
    ]
    args = [a, a2, w]
    if has_bias:
        in_specs.append(pl.BlockSpec((1, tn), lambda j, i, k: (0, j + j0)))
        args.append(bias.reshape(1, -1))
    if has_res:
        in_specs.append(pl.BlockSpec((tm, tn), lambda j, i, k: (i, j)))
        in_specs.append(pl.BlockSpec((SUBLANES, tn), lambda j, i, k: (0, j)))
        args += [res, res2]
    out_specs = [pl.BlockSpec((tm, tn), lambda j, i, k: (i, j)) for _ in out_dtypes]
    out_specs.append(pl.BlockSpec((SUBLANES, tn), lambda j, i, k: (0, j)))
    out_shape = [jax.ShapeDtypeStruct((m, ncols), d) for d in out_dtypes]
    out_shape.append(jax.ShapeDtypeStruct((SUBLANES, ncols), F32))
    scratch = []
    if b_f32:
        scratch.append(pltpu.VMEM((tk, tn), BF16))
    if nk > 1:
        scratch += [pltpu.VMEM((tm, tn), F32), pltpu.VMEM((SUBLANES, tn), F32)]
    outs = pl.pallas_call(
        functools.partial(_mm_kernel, nk=nk, b_f32=b_f32, has_bias=has_bias, act=act,
                          has_res=has_res, n_out=len(out_dtypes)),
        grid=(nj, ni, nk),
        in_specs=in_specs,
        out_specs=out_specs,
        out_shape=out_shape,
        scratch_shapes=scratch,
        compiler_params=_params("arbitrary", "arbitrary", "arbitrary"),
        name="proj_matmul",
    )(*args)
    return list(outs[:-1]), outs[-1]


def _shift_rows(x, prev8, k):
    xs = pltpu.roll(x, k, 0)
    pk = pltpu.roll(prev8, k, 0)
    row = _iota((SUBLANES, x.shape[1]), 0)
    top = jnp.where(row < k, pk, xs[:SUBLANES])
    return jnp.concatenate([top, xs[SUBLANES:]], axis=0)


def _causal_conv(x, prev8, w_ref, b_ref):
    width = w_ref.shape[0]
    acc = x * w_ref[width - 1:width, :] + b_ref[...]
    for k in range(1, width):
        acc = acc + _shift_rows(x, prev8, k) * w_ref[width - 1 - k:width - k, :]
    return acc


def _conv_rows(cur, prev_ref, w_ref, b_ref):
    width = w_ref.shape[0]
    acc = cur * w_ref[width - 1:width, :] + b_ref[...]
    for k in range(width - 1):
        acc = acc + prev_ref[k] * w_ref[k:k + 1, :]
    return acc


def _ssd_conv_kernel(x_ref, p_ref, w_ref, b_ref, o_ref, *, blocks_per_seq):
    r = pl.program_id(0)
    first = (r % blocks_per_seq) == 0
    prev8 = jnp.where(first, 0.0, p_ref[...])
    y = _causal_conv(x_ref[...], prev8, w_ref, b_ref)
    o_ref[...] = y * _sigmoid(y)


def _ssd_conv(zx, col0, conv_w, conv_b, seq):
    m = zx.shape[0]
    width, c = conv_w.shape
    tr, tc = min(seq, 512), 512
    assert seq % tr == 0 and c % tc == 0 and col0 % tc == 0
    j0 = col0 // tc
    return pl.pallas_call(
        functools.partial(_ssd_conv_kernel, blocks_per_seq=seq // tr),
        grid=(m // tr, c // tc),
        in_specs=[
            pl.BlockSpec((tr, tc), lambda r, j: (r, j + j0)),
            pl.BlockSpec((SUBLANES, tc), lambda r, j: (jnp.maximum(r * (tr // SUBLANES) - 1, 0), j + j0)),
            pl.BlockSpec((width, tc), lambda r, j: (0, j)),
            pl.BlockSpec((1, tc), lambda r, j: (0, j)),
        ],
        out_specs=pl.BlockSpec((tr, tc), lambda r, j: (r, j)),
        out_shape=jax.ShapeDtypeStruct((m, c), F32),
        compiler_params=_params("parallel", "parallel"),
        name="ssd_conv",
    )(zx, zx, conv_w, conv_b.reshape(1, c))


def _gated_group_norm(y, z, nw):
    g = y * (z * _sigmoid(z))
    ms = jnp.mean(g * g, axis=-1, keepdims=True)
    return (g * lax.rsqrt(ms + NORM_EPS)) * nw


def _ssd_scan_kernel(z_ref, x_ref, bm_ref, cm_ref, dt_ref, dtb_ref, alog_ref, dsk_ref, nw_ref,
                     g_ref, st_out_ref, st_ref, cumt_ref, dtt_ref, *, hpg, nchunk):
    g = pl.program_id(1)
    ci = pl.program_id(2)
    c = x_ref.shape[0]
    heads = dt_ref.shape[1]
    p_dim = x_ref.shape[1] // hpg

    @pl.when(ci == 0)
    def _():
        st_ref[...] = jnp.zeros_like(st_ref)

    dt = _softplus(dt_ref[...] + dtb_ref[...])
    dta = dt * (-jnp.exp(alog_ref[...]))
    tril = _one_hot(_iota((c, c), 0) >= _iota((c, c), 1))
    cum = _xdot_l(tril, dta)
    cumt_ref[...] = cum.T
    dtt_ref[...] = dt.T
    cum_last = cum[c - 1:c, :]
    gw = hpg * p_dim
    spread = _one_hot(_iota((heads, gw), 0) == g * hpg + _iota((heads, gw), 1) // p_dim)
    exp_cum = _xdot(jnp.exp(cum), spread)
    to_end = _xdot(jnp.exp(cum_last - cum) * dt, spread)
    dec_last = _xdot(_rows8(jnp.exp(cum_last)), spread)[0:1]

    x = x_ref[...]
    bm = bm_ref[...].astype(BF16)
    cm = cm_ref[...].astype(BF16)
    cb = _dot_nt(cm, bm)
    tri = _iota((c, c), 0) >= _iota((c, c), 1)
    heads_per_tile = LANES // p_dim
    lane = _iota((c, LANES), 1)
    tiles = []
    for tile in range(gw // LANES):
        xt = x[:, tile * LANES:(tile + 1) * LANES]
        acc = None
        for e in range(heads_per_tile):
            h = g * hpg + tile * heads_per_tile + e
            cs = cumt_ref[pl.ds(h, 1), :]
            ct = _col_bcast(cs, c)
            seg = ct - cs
            dec = jnp.where(tri, jnp.exp(jnp.where(tri, seg, 0.0)), 0.0)
            w = cb * (dec * dtt_ref[pl.ds(h, 1), :])
            xm = jnp.where((lane >= e * p_dim) & (lane < (e + 1) * p_dim), xt, 0.0)
            part = _dot(w.astype(BF16), xm.astype(BF16))
            acc = part if acc is None else acc + part
        tiles.append(acc)
    y = jnp.concatenate(tiles, axis=1) if len(tiles) > 1 else tiles[0]
    st = st_ref[...]
    y = y + _dot_nt(cm, st.astype(BF16)) * exp_cum + dsk_ref[...] * x
    g_ref[...] = _gated_group_norm(y, z_ref[...], nw_ref[...]).astype(g_ref.dtype)

    upd = _dot_tn((x * to_end).astype(BF16), bm)
    new = st * _col_bcast(dec_last, st.shape[1]) + upd
    st_ref[...] = new

    @pl.when(ci == nchunk - 1)
    def _():
        st_out_ref[0, 0] = new


def _ssd_scan(zx, xc, dt_raw, dt_bias, a_log, dskip_x, norm_w, *, batch, seq, groups, hpg, p_dim, n_state):
    m = zx.shape[0]
    c = math.gcd(seq, SSD_CHUNK)
    assert c == SSD_CHUNK, "sequence length must be a multiple of the SSD chunk"
    nchunk = seq // c
    gw = hpg * p_dim
    heads = groups * hpg
    d_inner = heads * p_dim
    assert gw % LANES == 0 and LANES % p_dim == 0 and n_state == LANES and heads == LANES
    nxb = d_inner // n_state
    row = lambda b, g, ci: b * nchunk + ci
    return pl.pallas_call(
        functools.partial(_ssd_scan_kernel, hpg=hpg, nchunk=nchunk),
        grid=(batch, groups, nchunk),
        in_specs=[
            pl.BlockSpec((c, gw), lambda b, g, ci: (row(b, g, ci), g)),
            pl.BlockSpec((c, gw), lambda b, g, ci: (row(b, g, ci), g)),
            pl.BlockSpec((c, n_state), lambda b, g, ci: (row(b, g, ci), nxb + g)),
            pl.BlockSpec((c, n_state), lambda b, g, ci: (row(b, g, ci), nxb + groups + g)),
            pl.BlockSpec((c, heads), lambda b, g, ci: (row(b, g, ci), 0)),
            pl.BlockSpec((1, heads), lambda b, g, ci: (0, 0)),
            pl.BlockSpec((1, heads), lambda b, g, ci: (0, 0)),
            pl.BlockSpec((1, gw), lambda b, g, ci: (0, g)),
            pl.BlockSpec((1, gw), lambda b, g, ci: (0, g)),
        ],
        out_specs=[
            pl.BlockSpec((c, gw), lambda b, g, ci: (row(b, g, ci), g)),
            pl.BlockSpec((1, 1, gw, n_state), lambda b, g, ci: (b, g, 0, 0)),
        ],
        out_shape=[
            jax.ShapeDtypeStruct((m, d_inner), BF16),
            jax.ShapeDtypeStruct((batch, groups, gw, n_state), F32),
        ],
        scratch_shapes=[pltpu.VMEM((gw, n_state), F32), pltpu.VMEM((heads, c), F32), pltpu.VMEM((heads, c), F32)],
        compiler_params=_params("parallel", "parallel", "arbitrary"),
        name="ssd_scan",
    )(zx, xc, xc, xc, dt_raw, dt_bias.reshape(1, -1), a_log.reshape(1, -1), dskip_x.reshape(1, -1),
      norm_w.reshape(1, -1))


def _ssd_conv_sample_kernel(x_ref, p_ref, w_ref, b_ref, o_ref):
    y = _conv_rows(x_ref[...], p_ref, w_ref, b_ref)
    o_ref[...] = y * _sigmoid(y)


def _ssd_conv_sample(proj, col0, prev_t, conv_w, conv_b):
    width, c = conv_w.shape
    tc = 512
    j0 = col0 // tc
    return pl.pallas_call(
        _ssd_conv_sample_kernel,
        grid=(c // tc,),
        in_specs=[
            pl.BlockSpec((SUBLANES, tc), lambda j: (0, j + j0)),
            pl.BlockSpec((width - 1, SUBLANES, tc), lambda j: (0, 0, j)),
            pl.BlockSpec((width, tc), lambda j: (0, j)),
            pl.BlockSpec((1, tc), lambda j: (0, j)),
        ],
        out_specs=pl.BlockSpec((SUBLANES, tc), lambda j: (0, j)),
        out_shape=jax.ShapeDtypeStruct((SUBLANES, c), F32),
        compiler_params=_params("parallel"),
        name="ssd_conv_sample",
    )(proj, prev_t, conv_w, conv_b.reshape(1, c))


def _ssd_step_kernel(z_ref, x_ref, bm_ref, cm_ref, dt_ref, dtb_ref, alog_ref, dsk_ref, nw_ref, st_ref,
                     g_ref, st_out_ref, *, hpg):
    g = pl.program_id(1)
    heads = dt_ref.shape[2]
    gw = x_ref.shape[2]
    p_dim = gw // hpg
    dt = _softplus(dt_ref[0] + dtb_ref[...])
    dta = dt * (-jnp.exp(alog_ref[...]))
    spread = _one_hot(_iota((heads, gw), 0) == g * hpg + _iota((heads, gw), 1) // p_dim)
    dec = _xdot(_rows8(jnp.exp(dta)), spread)[0:1]
    dtx = _xdot(_rows8(dt), spread)[0:1]
    x = x_ref[0]
    bm = bm_ref[0]
    cm = cm_ref[0]
    st = st_ref[0, 0]
    n_state = st.shape[1]
    cb = jnp.sum(cm * bm, axis=-1, keepdims=True)
    y_state = _dot_nt(jnp.broadcast_to(cm, (SUBLANES, n_state)).astype(BF16), st.astype(BF16))[0:1]
    y = cb * dtx * x + y_state * dec + dsk_ref[...] * x
    g_ref[0] = _gated_group_norm(y, z_ref[0], nw_ref[...]).astype(g_ref.dtype)
    st_out_ref[0, 0] = st * _col_bcast(dec, n_state) + _col_bcast(dtx * x, n_state) * bm


def _ssd_step(proj3, xc3, st, dt_bias, a_log, dskip_x, norm_w, *, groups, hpg, p_dim, n_state):
    nb = proj3.shape[0]
    gw = hpg * p_dim
    heads = groups * hpg
    d_inner = heads * p_dim
    nxb = d_inner // n_state
    dt_blk = (2 * d_inner + 2 * groups * n_state) // heads
    assert (2 * d_inner + 2 * groups * n_state) % heads == 0
    return pl.pallas_call(
        functools.partial(_ssd_step_kernel, hpg=hpg),
        grid=(nb, groups),
        in_specs=[
            pl.BlockSpec((1, 1, gw), lambda b, g: (b, 0, g)),
            pl.BlockSpec((1, 1, gw), lambda b, g: (b, 0, g)),
            pl.BlockSpec((1, 1, n_state), lambda b, g: (b, 0, nxb + g)),
            pl.BlockSpec((1, 1, n_state), lambda b, g: (b, 0, nxb + groups + g)),
            pl.BlockSpec((1, 1, heads), lambda b, g: (b, 0, dt_blk)),
            pl.BlockSpec((1, heads), lambda b, g: (0, 0)),
            pl.BlockSpec((1, heads), lambda b, g: (0, 0)),
            pl.BlockSpec((1, gw), lambda b, g: (0, g)),
            pl.BlockSpec((1, gw), lambda b, g: (0, g)),
            pl.BlockSpec((1, 1, gw, n_state), lambda b, g: (b, g, 0, 0)),
        ],
        out_specs=[
            pl.BlockSpec((1, 1, gw), lambda b, g: (b, 0, g)),
            pl.BlockSpec((1, 1, gw, n_state), lambda b, g: (b, g, 0, 0)),
        ],
        out_shape=[
            jax.ShapeDtypeStruct((nb, 1, d_inner), BF16),
            jax.ShapeDtypeStruct((nb, groups, gw, n_state), F32),
        ],
        compiler_params=_params("parallel", "parallel"),
        name="ssd_step",
    )(proj3, xc3, xc3, xc3, proj3, dt_bias.reshape(1, -1), a_log.reshape(1, -1), dskip_x.reshape(1, -1),
      norm_w.reshape(1, -1), st)


def _ffn_gate_kernel(g_ref, v_ref, pg_ref, pv_ref, wg_ref, wv_ref, bg_ref, bv_ref, o_ref, *, blocks_per_seq):
    r = pl.program_id(0)
    first = (r % blocks_per_seq) == 0
    gate = _causal_conv(g_ref[...], jnp.where(first, 0.0, pg_ref[...]), wg_ref, bg_ref)
    val = _causal_conv(v_ref[...], jnp.where(first, 0.0, pv_ref[...]), wv_ref, bv_ref)
    o_ref[...] = ((gate * _sigmoid(gate)) * val).astype(o_ref.dtype)


def _ffn_gate(up, conv_w, conv_b, seq):
    m, two_f = up.shape
    f = two_f // 2
    width = conv_w.shape[0]
    tr = min(seq, 1024)
    tc = next(t for t in (512, 256, 128) if f % t == 0)
    nc = f // tc
    assert seq % tr == 0
    prev_idx = lambda r: jnp.maximum(r * (tr // SUBLANES) - 1, 0)
    b2 = conv_b.reshape(1, two_f)
    return pl.pallas_call(
        functools.partial(_ffn_gate_kernel, blocks_per_seq=seq // tr),
        grid=(m // tr, nc),
        in_specs=[
            pl.BlockSpec((tr, tc), lambda r, j: (r, j)),
            pl.BlockSpec((tr, tc), lambda r, j: (r, j + nc)),
            pl.BlockSpec((SUBLANES, tc), lambda r, j: (prev_idx(r), j)),
            pl.BlockSpec((SUBLANES, tc), lambda r, j: (prev_idx(r), j + nc)),
            pl.BlockSpec((width, tc), lambda r, j: (0, j)),
            pl.BlockSpec((width, tc), lambda r, j: (0, j + nc)),
            pl.BlockSpec((1, tc), lambda r, j: (0, j)),
            pl.BlockSpec((1, tc), lambda r, j: (0, j + nc)),
        ],
        out_specs=pl.BlockSpec((tr, tc), lambda r, j: (r, j)),
        out_shape=jax.ShapeDtypeStruct((m, f), BF16),
        compiler_params=_params("parallel", "parallel"),
        name="ffn_gate",
    )(up, up, up, up, conv_w, conv_w, b2, b2)


def _ffn_gate_sample_kernel(g_ref, v_ref, pg_ref, pv_ref, wg_ref, wv_ref, bg_ref, bv_ref, o_ref):
    gate = _conv_rows(g_ref[...], pg_ref, wg_ref, bg_ref)
    val = _conv_rows(v_ref[...], pv_ref, wv_ref, bv_ref)
    o_ref[...] = ((gate * _sigmoid(gate)) * val).astype(o_ref.dtype)


def _ffn_gate_sample(up, prev_t, conv_w, conv_b):
    two_f = up.shape[1]
    f = two_f // 2
    width = conv_w.shape[0]
    tc = next(t for t in (512, 256, 128) if f % t == 0)
    nc = f // tc
    b2 = conv_b.reshape(1, two_f)
    return pl.pallas_call(
        _ffn_gate_sample_kernel,
        grid=(nc,),
        in_specs=[
            pl.BlockSpec((SUBLANES, tc), lambda j: (0, j)),
            pl.BlockSpec((SUBLANES, tc), lambda j: (0, j + nc)),
            pl.BlockSpec((width - 1, SUBLANES, tc), lambda j: (0, 0, j)),
            pl.BlockSpec((width - 1, SUBLANES, tc), lambda j: (0, 0, j + nc)),
            pl.BlockSpec((width, tc), lambda j: (0, j)),
            pl.BlockSpec((width, tc), lambda j: (0, j + nc)),
            pl.BlockSpec((1, tc), lambda j: (0, j)),
            pl.BlockSpec((1, tc), lambda j: (0, j + nc)),
        ],
        out_specs=pl.BlockSpec((SUBLANES, tc), lambda j: (0, j)),
        out_shape=jax.ShapeDtypeStruct((SUBLANES, f), BF16),
        compiler_params=_params("parallel"),
        name="ffn_gate_sample",
    )(up, up, prev_t, prev_t, conv_w, conv_w, b2, b2)


def _flash_kernel(*refs, mode, hb, dh, tq, scale, nblk):
    if mode == "fox":
        q_ref, k_ref, v_ref, o_ref, m_ref, l_ref, acc_ref = refs
        kw = 2 * dh
    else:
        q_ref, k_ref, v_ref, tt_ref, km_ref, o_ref, m_ref, l_ref, acc_ref, sel_ref = refs
        kw = dh
    qi = pl.program_id(2)
    c_exp = scale * math.log2(math.e)
    causal = _iota((tq, tq), 0) >= _iota((tq, tq), 1)

    def q_of(hh):
        q = q_ref[:, hh * dh:(hh + 1) * dh]
        if mode == "fox":
            ones = jnp.where(_iota((tq, dh), 1) < 3, 1.0, 0.0).astype(BF16)
            q = jnp.concatenate([q, ones], axis=1)
        return q

    def scores(hh, ks):
        return _dot_nt(q_of(hh), k_ref[pl.ds(ks, tq), hh * kw:(hh + 1) * kw])

    def update(hh, u, ks, first):
        v = v_ref[pl.ds(ks, tq), hh * dh:(hh + 1) * dh]
        rmax = jnp.max(u, axis=-1, keepdims=True)
        if first:
            m_new = jnp.broadcast_to(rmax, (tq, LANES))
        else:
            m_old = m_ref[hh]
            m_new = jnp.maximum(m_old, rmax)
            alpha = jnp.exp2((m_old - m_new) * c_exp)
        p = jnp.exp2((u - jnp.concatenate([m_new] * (tq // LANES), axis=1)) * c_exp)
        psum = jnp.sum(p, axis=-1, keepdims=True)
        pv = _dot(p.astype(BF16), v)
        if first:
            l_ref[hh] = jnp.broadcast_to(psum, (tq, LANES))
            acc_ref[hh] = pv
        else:
            l_ref[hh] = alpha * l_ref[hh] + psum
            acc_ref[hh] = alpha * acc_ref[hh] + pv
        m_ref[hh] = m_new

    ks0 = pl.multiple_of(qi * tq, tq)
    for hh in range(hb):
        u = scores(hh, ks0)
        if mode == "moba":
            u = u + tt_ref[hh, 0]
            q = q_ref[:, hh * dh:(hh + 1) * dh]
            nb8 = -(-nblk // SUBLANES) * SUBLANES
            gate = sum(_dot_nt(part, q) for part in _split(km_ref[hh], 3))[:nb8]
            blk = _iota((nb8, tq), 0)
            valid = blk < qi
            gate = jnp.where(valid, gate, NEG_INF)
            rank = jnp.zeros((nb8, tq), F32)
            for mth in range(nblk):
                gm = gate[mth:mth + 1, :]
                beats = (gm > gate) | ((gm == gate) & (mth < blk))
                rank = rank + jnp.where(beats, 1.0, 0.0)
            keep = jnp.where(valid & (rank < float(min(MOBA_TOPK, nblk))), 1.0, 0.0)
            keep = jnp.concatenate([keep, jnp.zeros((LANES - nb8, tq), F32)], axis=0)
            sel_ref[hh] = keep.T.astype(BF16)
        update(hh, jnp.where(causal, u, NEG_INF), ks0, True)

    def body(j, carry):
        ki = qi - j
        ks = pl.multiple_of(ki * tq, tq)
        for hh in range(hb):
            u = scores(hh, ks)
            if mode == "moba":
                pick = _one_hot(_iota((LANES, tq), 0) == ki)
                u = jnp.where(_dot(sel_ref[hh], pick) > 0.5, u + tt_ref[hh, j], NEG_INF)
            update(hh, u, ks, False)
        return carry

    lax.fori_loop(1, qi + 1, body, 0)
    for hh in range(hb):
        o_ref[:, hh * dh:(hh + 1) * dh] = (acc_ref[hh] / l_ref[hh]).astype(o_ref.dtype)


def _flash(q, k, v, *, mode, batch, seq, heads, dh, scale, tt=None, kmean=None):
    tq = MOBA_BLOCK
    hb = 4 if heads % 4 == 0 else (2 if heads % 2 == 0 else 1)
    assert seq % tq == 0 and dh == LANES
    nq = seq // tq
    ngrp = heads // hb
    kw = k.shape[1] // heads
    in_specs = [
        pl.BlockSpec((tq, hb * dh), lambda b, hg, qi: (b * nq + qi, hg)),
        pl.BlockSpec((seq, hb * kw), lambda b, hg, qi: (b, hg)),
        pl.BlockSpec((seq, hb * dh), lambda b, hg, qi: (b, hg)),
    ]
    args = [q, k, v]
    scratch = [pltpu.VMEM((hb, tq, LANES), F32), pltpu.VMEM((hb, tq, LANES), F32), pltpu.VMEM((hb, tq, dh), F32)]
    if mode == "moba":
        nd = tt.shape[1]
        in_specs.append(pl.BlockSpec((hb, nd, tq, tq), lambda b, hg, qi: (hg, 0, 0, 0)))
        in_specs.append(pl.BlockSpec((hb, LANES, dh), lambda b, hg, qi: (b * ngrp + hg, 0, 0)))
        args += [tt, kmean]
        scratch.append(pltpu.VMEM((hb, tq, LANES), BF16))
    return pl.pallas_call(
        functools.partial(_flash_kernel, mode=mode, hb=hb, dh=dh, tq=tq, scale=scale, nblk=seq // MOBA_BLOCK),
        grid=(batch, ngrp, nq),
        in_specs=in_specs,
        out_specs=pl.BlockSpec((tq, hb * dh), lambda b, hg, qi: (b * nq + qi, hg)),
        out_shape=jax.ShapeDtypeStruct(q.shape, BF16),
        scratch_shapes=scratch,
        compiler_params=_params("parallel", "parallel", "arbitrary"),
        name="flash_" + mode,
    )(*args)


def _fox_cumsum_kernel(lf_ref, cum_ref, *, seq):
    parts = _split(lf_ref[...], 3)
    tr = min(seq, 512)
    for j in range(seq // tr):
        upto = _one_hot(_iota((tr, seq), 1) <= _iota((tr, seq), 0) + j * tr)
        cum_ref[j * tr:(j + 1) * tr, :] = sum(_dot(upto, p) for p in parts)


def _fox_cumsum(lf, batch, seq):
    heads = lf.shape[1]
    return pl.pallas_call(
        functools.partial(_fox_cumsum_kernel, seq=seq),
        grid=(batch,),
        in_specs=[pl.BlockSpec((seq, heads), lambda b: (b, 0))],
        out_specs=pl.BlockSpec((seq, heads), lambda b: (b, 0)),
        out_shape=jax.ShapeDtypeStruct(lf.shape, F32),
        compiler_params=_params("parallel"),
        name="fox_cumsum",
    )(lf)


def _fox_keys_kernel(k_ref, cum_ref, o_ref, *, hb, dh, heads, inv_scale):
    hg = pl.program_id(1)
    pieces = _split(cum_ref[...] * (-inv_scale), 3)
    src = _iota((heads, dh), 0)
    dst = _iota((heads, dh), 1)
    for hh in range(hb):
        h = hg * hb + hh
        extra = sum(_dot(pieces[i], _one_hot((dst == i) & (src == h))) for i in range(3))
        o_ref[:, hh * 2 * dh:hh * 2 * dh + dh] = k_ref[:, hh * dh:(hh + 1) * dh]
        o_ref[:, hh * 2 * dh + dh:(hh + 1) * 2 * dh] = extra.astype(BF16)


def _fox_keys(k, cum, batch, seq, heads, dh, scale):
    hb = 4 if heads % 4 == 0 else (2 if heads % 2 == 0 else 1)
    assert 3 * heads <= LANES
    return pl.pallas_call(
        functools.partial(_fox_keys_kernel, hb=hb, dh=dh, heads=heads, inv_scale=1.0 / scale),
        grid=(batch, heads // hb),
        in_specs=[pl.BlockSpec((seq, hb * dh), lambda b, hg: (b, hg)), pl.BlockSpec((seq, heads), lambda b, hg: (b, 0))],
        out_specs=pl.BlockSpec((seq, hb * 2 * dh), lambda b, hg: (b, hg)),
        out_shape=jax.ShapeDtypeStruct((batch * seq, heads * 2 * dh), BF16),
        compiler_params=_params("parallel", "parallel"),
        name="fox_keys",
    )(k, cum)


def _t5_bucket(rel, n_buckets):
    exact = n_buckets // 2
    n = jnp.maximum(rel, 0)
    nf = jnp.maximum(n, 1).astype(F32)
    large = exact + (jnp.log(nf / exact) / math.log(REL_MAX_DIST / exact) * (n_buckets - exact)).astype(jnp.int32)
    return jnp.where(n < exact, n, jnp.minimum(large, n_buckets - 1))


def _rel_tile_kernel(rb_ref, o_ref, *, n_buckets, tq, inv_scale):
    h = pl.program_id(0)
    d = pl.program_id(1)
    rel = d * tq + _iota((tq, tq), 0) - _iota((tq, tq), 1)
    bucket = _t5_bucket(rel, n_buckets)
    out = jnp.zeros((tq, tq), F32)
    for j in range(n_buckets):
        out = jnp.where(bucket == j, rb_ref[h * n_buckets + j], out)
    o_ref[0, 0] = out * inv_scale


def _rel_tiles(rel_bias, nd, inv_scale):
    n_buckets, heads = rel_bias.shape
    tq = MOBA_BLOCK
    return pl.pallas_call(
        functools.partial(_rel_tile_kernel, n_buckets=n_buckets, tq=tq, inv_scale=inv_scale),
        grid=(heads, nd),
        in_specs=[pl.BlockSpec(memory_space=pltpu.SMEM)],
        out_specs=pl.BlockSpec((1, 1, tq, tq), lambda h, d: (h, d, 0, 0)),
        out_shape=jax.ShapeDtypeStruct((heads, nd, tq, tq), F32),
        compiler_params=_params("parallel", "parallel"),
        name="t5_bias_tiles",
    )(rel_bias.T.reshape(-1))


def _kmean_kernel(k_ref, o_ref, *, nblk, hb, dh):
    blk = MOBA_BLOCK
    rows = _iota((LANES, hb * dh), 0)
    out = jnp.zeros((LANES, hb * dh), F32)
    for n in range(nblk):
        mean = jnp.sum(k_ref[n * blk:(n + 1) * blk, :], axis=0, keepdims=True) * (1.0 / blk)
        out = jnp.where(rows == n, mean, out)
    for hh in range(hb):
        o_ref[hh] = out[:, hh * dh:(hh + 1) * dh]


def _kmean(k, batch, seq, heads, dh):
    nblk = seq // MOBA_BLOCK
    hb = 4 if heads % 4 == 0 else (2 if heads % 2 == 0 else 1)
    ngrp = heads // hb
    assert nblk <= LANES
    return pl.pallas_call(
        functools.partial(_kmean_kernel, nblk=nblk, hb=hb, dh=dh),
        grid=(batch, ngrp),
        in_specs=[pl.BlockSpec((seq, hb * dh), lambda b, hg: (b, hg))],
        out_specs=pl.BlockSpec((hb, LANES, dh), lambda b, hg: (b * ngrp + hg, 0, 0)),
        out_shape=jax.ShapeDtypeStruct((batch * heads, LANES, dh), F32),
        compiler_params=_params("parallel", "parallel"),
        name="moba_kmean",
    )(k)


def _diag_slabs(x3, heads):
    t, _, w = x3.shape
    eye = _iota((t, heads, w), 1) == _iota((t, heads, w), 2)
    return jnp.where(eye, jnp.broadcast_to(x3, (t, heads, w)), 0.0)


def _decode_kernel(*refs, mode, npages, scale):
    if mode == "fox":
        pt_ref, q_ref, kn_ref, vn_ref, k_ref, v_ref, lf_ref, lfn_ref, o_ref, m_ref, l_ref, acc_ref, carry_ref = refs
    else:
        pt_ref, q_ref, kn_ref, vn_ref, k_ref, v_ref, tbl_ref, sel_ref, rb_ref, o_ref, m_ref, l_ref, acc_ref = refs
    p = pl.program_id(1)
    qs = q_ref[...] * scale
    heads = qs.shape[1]

    @pl.when(p == 0)
    def _():
        s0 = qs * kn_ref[...]
        if mode == "fox":
            carry_ref[...] = lfn_ref[0]
        else:
            s0 = s0 + _diag_slabs(rb_ref[0:1, :][None], heads)
        s0 = jnp.sum(s0, axis=-1, keepdims=True)
        m_ref[...] = s0
        l_ref[...] = jnp.ones_like(s0)
        acc_ref[...] = vn_ref[...]

    ps = k_ref.shape[1]
    if mode == "fox":
        lf = lf_ref[0]
        run = carry_ref[...]
        rows = [None] * ps
        for t in reversed(range(ps)):
            rows[t] = run
            run = run + lf[t:t + 1, :]
        carry_ref[...] = run
    else:
        tbl = tbl_ref[0]
        rows = [tbl[t:t + 1, :] for t in range(ps)]
    bias = _diag_slabs(jnp.concatenate([r[None] for r in rows], axis=0), heads)
    s = jnp.sum(k_ref[0] * qs + bias, axis=-1, keepdims=True)
    if mode == "moba":
        s = jnp.where(sel_ref[0] > 0.5, s, NEG_INF)
    m_old = m_ref[...]
    m_new = jnp.maximum(m_old, jnp.max(s, axis=0, keepdims=True))
    alpha = jnp.exp(m_old - m_new)
    pm = jnp.exp(s - m_new)
    l_new = alpha * l_ref[...] + jnp.sum(pm, axis=0, keepdims=True)
    acc = alpha * acc_ref[...] + jnp.sum(pm * v_ref[0], axis=0, keepdims=True)
    m_ref[...] = m_new
    l_ref[...] = l_new
    acc_ref[...] = acc

    @pl.when(p == npages - 1)
    def _():
        o_ref[...] = (acc / l_new).astype(o_ref.dtype)


def _decode_attn(mode, q, k_new, v_new, k_pool, v_pool, page_table, *, scale,
                 lf_pool=None, lf_new=None, bias_tbl=None, sel=None, rel_bias=None):
    nb, npages = page_table.shape
    _, ps, heads, dh = k_pool.shape
    rev = lambda p: npages - 1 - p
    page = lambda b, p, pt: pt[b * npages + rev(p)]
    tok = pl.BlockSpec((1, heads, dh), lambda b, p, pt: (b, 0, 0))
    pages = pl.BlockSpec((1, ps, heads, dh), lambda b, p, pt: (page(b, p, pt), 0, 0, 0))
    in_specs = [tok, tok, tok, pages, pages]
    args = [q, k_new, v_new, k_pool, v_pool]
    scratch = [pltpu.VMEM((1, heads, 1), F32), pltpu.VMEM((1, heads, 1), F32), pltpu.VMEM((1, heads, dh), F32)]
    assert heads <= dh
    if mode == "fox":
        in_specs.append(pl.BlockSpec((1, ps, dh), lambda b, p, pt: (page(b, p, pt), 0, 0)))
        in_specs.append(pl.BlockSpec((1, 1, dh), lambda b, p, pt: (b, 0, 0)))
        args += [lf_pool, lf_new.reshape(nb, 1, dh)]
        scratch.append(pltpu.VMEM((1, dh), F32))
    else:
        ppb = MOBA_BLOCK // ps
        in_specs.append(pl.BlockSpec((1, ps, dh), lambda b, p, pt: (rev(p), 0, 0)))
        in_specs.append(pl.BlockSpec((1, 1, heads, 1), lambda b, p, pt: (b, rev(p) // ppb, 0, 0)))
        in_specs.append(pl.BlockSpec(rel_bias.shape, lambda b, p, pt: (0, 0)))
        args += [bias_tbl, sel, rel_bias]
    return pl.pallas_call(
        functools.partial(_decode_kernel, mode=mode, npages=npages, scale=scale),
        grid_spec=pltpu.PrefetchScalarGridSpec(
            num_scalar_prefetch=1,
            grid=(nb, npages),
            in_specs=in_specs,
            out_specs=pl.BlockSpec((1, heads, dh), lambda b, p, pt: (b, 0, 0)),
            scratch_shapes=scratch,
        ),
        out_shape=jax.ShapeDtypeStruct((nb, heads, dh), BF16),
        compiler_params=_params("parallel", "arbitrary"),
        name="decode_" + mode,
    )(page_table.reshape(-1), *args)


def _moba_select_kernel(pt_ref, q_ref, k_ref, o_ref, ksum_ref, gate_ref, *, npages, ppb, nblk):
    p = pl.program_id(1)

    @pl.when(lax.rem(p, ppb) == 0)
    def _():
        ksum_ref[...] = jnp.zeros_like(ksum_ref)

    ksum_ref[...] += jnp.sum(k_ref[0], axis=0)

    @pl.when(lax.rem(p, ppb) == ppb - 1)
    def _():
        kmean = ksum_ref[...] * (1.0 / MOBA_BLOCK)
        gate_ref[pl.ds(lax.div(p, ppb), 1)] = jnp.sum(kmean * q_ref[0], axis=-1, keepdims=True)[None]

    @pl.when(p == npages - 1)
    def _():
        gate = gate_ref[...]
        blk = _iota(gate.shape, 0)
        rank = jnp.zeros(gate.shape, F32)
        for mth in range(nblk):
            gm = gate[mth:mth + 1]
            beats = (gm > gate) | ((gm == gate) & (mth < blk))
            rank = rank + jnp.where(beats, 1.0, 0.0)
        o_ref[0] = jnp.where(rank < float(min(MOBA_TOPK, nblk + 1)), 1.0, 0.0)


def _moba_select(q, k_pool, page_table):
    nb, npages = page_table.shape
    _, ps, heads, dh = k_pool.shape
    ppb = MOBA_BLOCK // ps
    nblk = npages // ppb
    assert MOBA_BLOCK % ps == 0 and npages % ppb == 0, "past length must be whole MoBA blocks"
    return pl.pallas_call(
        functools.partial(_moba_select_kernel, npages=npages, ppb=ppb, nblk=nblk),
        grid_spec=pltpu.PrefetchScalarGridSpec(
            num_scalar_prefetch=1,
            grid=(nb, npages),
            in_specs=[
                pl.BlockSpec((1, heads, dh), lambda b, p, pt: (b, 0, 0)),
                pl.BlockSpec((1, ps, heads, dh), lambda b, p, pt: (pt[b * npages + p], 0, 0, 0)),
            ],
            out_specs=pl.BlockSpec((1, nblk, heads, 1), lambda b, p, pt: (b, 0, 0, 0)),
            scratch_shapes=[pltpu.VMEM((heads, dh), F32), pltpu.VMEM((nblk, heads, 1), F32)],
        ),
        out_shape=jax.ShapeDtypeStruct((nb, nblk, heads, 1), F32),
        compiler_params=_params("parallel", "arbitrary"),
        name="moba_select",
    )(page_table.reshape(-1), q, k_pool)


def _rel_rows_kernel(rb_ref, o_ref, *, past, n_buckets):
    pg = pl.program_id(0)
    ps = o_ref.shape[1]
    kpos = pg * ps + _iota((ps, n_buckets), 0)
    bucket = _t5_bucket(past - kpos, n_buckets)
    hot = _one_hot(bucket == _iota((ps, n_buckets), 1))
    o_ref[0] = _xdot_l(hot, rb_ref[...])


def _rel_rows(rel_bias, npages, ps):
    n_buckets, heads = rel_bias.shape
    return pl.pallas_call(
        functools.partial(_rel_rows_kernel, past=npages * ps, n_buckets=n_buckets),
        grid=(npages,),
        in_specs=[pl.BlockSpec((n_buckets, heads), lambda p: (0, 0))],
        out_specs=pl.BlockSpec((1, ps, heads), lambda p: (p, 0, 0)),
        out_shape=jax.ShapeDtypeStruct((npages, ps, heads), F32),
        compiler_params=_params("parallel"),
        name="t5_bias_rows",
    )(rel_bias)


def _layernorm(v, g, b):
    mu = jnp.mean(v, axis=-1, keepdims=True)
    vc = v - mu
    var = jnp.mean(vc * vc, axis=-1, keepdims=True)
    return (vc * lax.rsqrt(var + NORM_EPS)) * g + b


def _gmlp_kernel(u_ref, v_ref, lg_ref, lb_ref, w_ref, bsp_ref, o_ref, *, ngroups):
    c = u_ref.shape[0]
    gw = u_ref.shape[1] // ngroups
    vn = _layernorm(v_ref[...], lg_ref[...], lb_ref[...])
    tri = _iota((c, c), 0) >= _iota((c, c), 1)
    for g in range(ngroups):
        w = jnp.where(tri, w_ref[g], 0.0).astype(BF16)
        s = _dot(w, vn[:, g * gw:(g + 1) * gw].astype(BF16)) + bsp_ref[g]
        o_ref[:, g * gw:(g + 1) * gw] = (u_ref[:, g * gw:(g + 1) * gw] * s).astype(o_ref.dtype)


def _gmlp_gate(z, ln_g, ln_b, w_sp, b_sp, seq):
    m = z.shape[0]
    half = z.shape[1] // 2
    ngroups, c, _ = w_sp.shape
    assert seq % c == 0
    return pl.pallas_call(
        functools.partial(_gmlp_kernel, ngroups=ngroups),
        grid=(m // c,),
        in_specs=[
            pl.BlockSpec((c, half), lambda i: (i, 0)),
            pl.BlockSpec((c, half), lambda i: (i, 1)),
            pl.BlockSpec((1, half), lambda i: (0, 0)),
            pl.BlockSpec((1, half), lambda i: (0, 0)),
            pl.BlockSpec((ngroups, c, c), lambda i: (0, 0, 0)),
            pl.BlockSpec((ngroups, c, 1), lambda i: (0, 0, 0)),
        ],
        out_specs=pl.BlockSpec((c, half), lambda i: (i, 0)),
        out_shape=jax.ShapeDtypeStruct((m, half), BF16),
        compiler_params=_params("parallel"),
        name="gmlp_gate",
    )(z, z, ln_g.reshape(1, half), ln_b.reshape(1, half), w_sp, b_sp.reshape(ngroups, c, 1))


def _gmlp_sample_kernel(u_ref, v_ref, lg_ref, lb_ref, w0_ref, b0_ref, o_ref, vn_ref):
    vn = _layernorm(v_ref[...], lg_ref[...], lb_ref[...])
    vn_ref[...] = vn
    o_ref[...] = (u_ref[...] * (vn * w0_ref[...] + b0_ref[...])).astype(o_ref.dtype)


def _gmlp_gate_sample(z, ln_g, ln_b, w0_x, b0_x):
    half = z.shape[1] // 2
    row = pl.BlockSpec((1, half), lambda i: (0, 0))
    return pl.pallas_call(
        _gmlp_sample_kernel,
        grid=(1,),
        in_specs=[pl.BlockSpec((SUBLANES, half), lambda i: (0, 0)), pl.BlockSpec((SUBLANES, half), lambda i: (0, 1)),
                  row, row, row, row],
        out_specs=[pl.BlockSpec((SUBLANES, half), lambda i: (0, 0)), pl.BlockSpec((SUBLANES, half), lambda i: (0, 0))],
        out_shape=[jax.ShapeDtypeStruct((SUBLANES, half), BF16), jax.ShapeDtypeStruct((SUBLANES, half), F32)],
        compiler_params=_params("arbitrary"),
        name="gmlp_gate_sample",
    )(z, z, ln_g.reshape(1, half), ln_b.reshape(1, half), w0_x.reshape(1, half), b0_x.reshape(1, half))


def kernel(x_prompt, x_sample, state_ssd_conv, state_ssd, cache_fox_k, cache_fox_v, cache_fox_logf, cache_moba_k, cache_moba_v, state_ffn_conv, page_table, norm_mix, norm_ffn, norm_final, ssd_w_in, ssd_conv_w, ssd_conv_b, ssd_dt_bias, ssd_a_log, ssd_d, ssd_norm, ssd_w_out, fox_w_qkv, fox_w_f, fox_b_f, fox_w_o, moba_w_qkv, moba_w_o, rel_bias, gmlp_w_in, gmlp_b_in, gmlp_ln_g, gmlp_ln_b, gmlp_w_sp, gmlp_b_sp, gmlp_w_out, ffn_w_up, ffn_conv_w, ffn_conv_b, ffn_w_down):
    batch, seq, d_model = x_prompt.shape
    nb, dec_seq, _ = x_sample.shape
    depth = norm_mix.shape[0]
    assert nb == SUBLANES and dec_seq == 1, "sample group must be 8 single-token sequences"
    m = batch * seq
    n_pages, page_size = page_table.shape[1], cache_fox_k.shape[2]
    past = n_pages * page_size

    _, _, ssd_heads, ssd_p, ssd_n = state_ssd.shape
    d_inner = ssd_heads * ssd_p
    conv_dim = ssd_conv_w.shape[2]
    ssd_groups = (conv_dim - d_inner) // (2 * ssd_n)
    hpg = ssd_heads // ssd_groups
    fox_heads, fox_dh = cache_fox_k.shape[3], cache_fox_k.shape[4]
    moba_heads, moba_dh = cache_moba_k.shape[3], cache_moba_k.shape[4]
    d_ff = ffn_w_down.shape[1]

    xp = x_prompt.reshape(m, d_model)
    xs = x_sample.reshape(nb, d_model)
    w_down_bf = ffn_w_down.astype(BF16)
    outs = {k: [] for k in ("p_ssd_conv", "p_ssd_state", "p_fox_k", "p_fox_v", "p_fox_lf", "p_moba_k", "p_moba_v",
                            "p_ffn", "s_ssd_conv", "s_ssd_state", "s_fox_k", "s_fox_v", "s_fox_lf", "s_moba_k",
                            "s_moba_v", "s_gmlp_v", "s_ffn")}

    def tail_rows(a, n):
        return a.reshape(batch, seq, a.shape[-1])[:, seq - n:]

    def lane_pad(a, width):
        return jnp.pad(a, [(0, 0)] * (a.ndim - 1) + [(0, width - a.shape[-1])])

    for i in range(depth):
        mixer, j = i % 4, i // 4
        hp = _rmsnorm(xp, norm_mix[i], BF16)
        hs = _rmsnorm(xs, norm_mix[i], BF16)
        if mixer == 0:
            n_zx = d_inner + conv_dim
            (zx,), zx_s = _matmul(hp, hs, ssd_w_in[j], col0=0, ncols=n_zx)
            (dt_raw,), dt_s = _matmul(hp, hs, ssd_w_in[j], col0=n_zx, ncols=ssd_heads)
            dskip_x = jnp.repeat(ssd_d[j], ssd_p)
            xc = _ssd_conv(zx, d_inner, ssd_conv_w[j], ssd_conv_b[j], seq)
            gp, st_p = _ssd_scan(zx, xc, dt_raw, ssd_dt_bias[j], ssd_a_log[j], dskip_x, ssd_norm[j], batch=batch,
                                 seq=seq, groups=ssd_groups, hpg=hpg, p_dim=ssd_p, n_state=ssd_n)
            outs["p_ssd_conv"].append(tail_rows(zx[:, d_inner:], ssd_conv_w.shape[1] - 1))
            outs["p_ssd_state"].append(st_p.reshape(batch, ssd_heads, ssd_p, ssd_n))
            prev = state_ssd_conv[j]
            xc_s = _ssd_conv_sample(zx_s, d_inner, prev.transpose(1, 0, 2), ssd_conv_w[j], ssd_conv_b[j])
            proj_s = jnp.concatenate([zx_s, dt_s], axis=1)
            gs, st_s = _ssd_step(proj_s.reshape(nb, 1, -1), xc_s.reshape(nb, 1, conv_dim),
                                 state_ssd[j].reshape(nb, ssd_groups, hpg * ssd_p, ssd_n), ssd_dt_bias[j],
                                 ssd_a_log[j], dskip_x, ssd_norm[j], groups=ssd_groups, hpg=hpg, p_dim=ssd_p,
                                 n_state=ssd_n)
            outs["s_ssd_conv"].append(jnp.concatenate([prev[:, 1:], zx_s[:, None, d_inner:]], axis=1))
            outs["s_ssd_state"].append(st_s.reshape(nb, ssd_heads, ssd_p, ssd_n))
            (xp,), xs = _matmul(gp, gs.reshape(nb, d_inner), ssd_w_out[j].astype(BF16), res=xp, res2=xs)
        elif mixer in (1, 2):
            fox = mixer == 1
            w_qkv, w_o = (fox_w_qkv[j], fox_w_o[j]) if fox else (moba_w_qkv[j], moba_w_o[j])
            heads, dh = (fox_heads, fox_dh) if fox else (moba_heads, moba_dh)
            hd = heads * dh
            scale = dh ** -0.5
            (q,), q_s = _matmul(hp, hs, w_qkv, col0=0, ncols=hd, out_dtypes=(BF16,))
            (k, kb), k_s = _matmul(hp, hs, w_qkv, col0=hd, ncols=hd, out_dtypes=(F32, BF16))
            (v, vb), v_s = _matmul(hp, hs, w_qkv, col0=2 * hd, ncols=hd, out_dtypes=(F32, BF16))
            pk = "fox" if fox else "moba"
            outs[f"p_{pk}_k"].append(k.reshape(batch, seq, heads, dh))
            outs[f"p_{pk}_v"].append(v.reshape(batch, seq, heads, dh))
            outs[f"s_{pk}_k"].append(k_s.reshape(nb, 1, heads, dh))
            outs[f"s_{pk}_v"].append(v_s.reshape(nb, 1, heads, dh))
            tok3 = [t.reshape(nb, heads, dh) for t in (q_s, k_s, v_s)]
            if fox:
                (lf,), lf_s = _matmul(hp, hs, fox_w_f[j], bias=fox_b_f[j], act="log_sigmoid")
                k_aug = _fox_keys(kb, _fox_cumsum(lf, batch, seq), batch, seq, heads, dh, scale)
                outs["p_fox_lf"].append(lf.reshape(batch, seq, heads))
                outs["s_fox_lf"].append(lf_s.reshape(nb, 1, heads))
                op = _flash(q, k_aug, vb, mode="fox", batch=batch, seq=seq, heads=heads, dh=dh, scale=scale)
                os_ = _decode_attn("fox", *tok3, cache_fox_k[j], cache_fox_v[j], page_table, scale=scale,
                                   lf_pool=lane_pad(cache_fox_logf[j], dh), lf_new=lane_pad(lf_s, dh))
            else:
                tt = _rel_tiles(rel_bias, seq // MOBA_BLOCK, 1.0 / scale)
                km = _kmean(k, batch, seq, heads, dh)
                op = _flash(q, kb, vb, mode="moba", batch=batch, seq=seq, heads=heads, dh=dh, scale=scale, tt=tt,
                            kmean=km)
                sel = _moba_select(tok3[0], cache_moba_k[j], page_table)
                os_ = _decode_attn("moba", *tok3, cache_moba_k[j], cache_moba_v[j], page_table, scale=scale,
                                   bias_tbl=_rel_rows(lane_pad(rel_bias, dh), n_pages, page_size), sel=sel,
                                   rel_bias=lane_pad(rel_bias, dh))
            (xp,), xs = _matmul(op, os_.reshape(nb, hd), w_o, res=xp, res2=xs)
        else:
            (z,), z_s = _matmul(hp, hs, gmlp_w_in[j], bias=gmlp_b_in[j], act="gelu")
            half = z.shape[1] // 2
            gp = _gmlp_gate(z, gmlp_ln_g[j], gmlp_ln_b[j], gmlp_w_sp[j], gmlp_b_sp[j], seq)
            gw = half // gmlp_w_sp.shape[1]
            gs, vn_s = _gmlp_gate_sample(z_s, gmlp_ln_g[j], gmlp_ln_b[j], jnp.repeat(gmlp_w_sp[j, :, 0, 0], gw),
                                         jnp.repeat(gmlp_b_sp[j, :, 0], gw))
            outs["s_gmlp_v"].append(vn_s.reshape(nb, 1, half))
            (xp,), xs = _matmul(gp, gs, gmlp_w_out[j].astype(BF16), res=xp, res2=xs)

        hp = _rmsnorm(xp, norm_ffn[i], BF16)
        hs = _rmsnorm(xs, norm_ffn[i], BF16)
        (up,), up_s = _matmul(hp, hs, ffn_w_up, layer=i)
        act = _ffn_gate(up, ffn_conv_w[i], ffn_conv_b[i], seq)
        prev = state_ffn_conv[i]
        act_s = _ffn_gate_sample(up_s, prev.transpose(1, 0, 2), ffn_conv_w[i], ffn_conv_b[i])
        outs["p_ffn"].append(tail_rows(up, ffn_conv_w.shape[1] - 1))
        outs["s_ffn"].append(jnp.concatenate([prev[:, 1:], up_s[:, None]], axis=1))
        (xp,), xs = _matmul(act, act_s, w_down_bf, layer=i, res=xp, res2=xs)

    y_prompt = _rmsnorm(xp, norm_final, F32).reshape(batch, seq, d_model)
    y_sample = _rmsnorm(xs, norm_final, F32).reshape(nb, 1, d_model)
    st = lambda key: jnp.stack(outs[key])
    return (y_prompt, y_sample,
            st("p_ssd_conv"), st("p_ssd_state"), st("p_fox_k"), st("p_fox_v"), st("p_fox_lf"), st("p_moba_k"),
            st("p_moba_v"), st("p_ffn"),
            st("s_ssd_conv"), st("s_ssd_state"), st("s_fox_k"), st("s_fox_v"), st("s_fox_lf"), st("s_moba_k"),
            st("s_moba_v"), st("s_gmlp_v"), st("s_ffn"))
```
